```python
import math
import jax, jax.numpy as jnp
from jax import lax
import numpy as np

D_MODEL = 2048
BATCH = 8
SEQ = 2048
DEPTH = 4

GRID_W = 64
CTX_LEN = 256
N_MIXERS = 2
N_MODS = 6
RMS_EPS = 1e-6
NEG_INF = -1e30
HEAD_DIM = 128
N_Q_HEADS = D_MODEL // HEAD_DIM
N_KV_HEADS = 4
GQA_GROUP = N_Q_HEADS // N_KV_HEADS
WINDOW = 128
ATTN_BLOCK = 128
ROPE_BASE = 10000.0
HY_ORDER = 2
HY_SHORT_W = 3
HY_EMB_BANDS = 16
HY_EMB_DIM = 1 + 2 * HY_EMB_BANDS
HY_FILTER_HIDDEN = 64
HY_INNER_MLPS = 2
HY_MOD_SHIFT = 0.05
HY_N_DIR = 2
PEER_HEADS = 8
PEER_N_KEYS = 128
PEER_N_EXPERTS = PEER_N_KEYS * PEER_N_KEYS
PEER_D_KEY = 256
PEER_HALF = PEER_D_KEY // 2
PEER_TOPK = 16
PEER_CHUNK = 128

N_ATTN_LAYERS = (DEPTH + N_MIXERS - 1) // N_MIXERS
N_HYENA_LAYERS = DEPTH // N_MIXERS

kernel_name = "hybrid_swa_hyena_peer_dit"


def rms_norm(x, g):
    xf = x.astype(jnp.float32)
    y = xf * lax.rsqrt(jnp.mean(xf * xf, axis=-1, keepdims=True) + RMS_EPS)
    return (y * g.astype(jnp.float32)).astype(x.dtype)


def modulate(h, g, shift, scale):
    return rms_norm(h, g) * (1 + scale) + shift


def axial_rope_tables(seq_len):
    rows = seq_len // GRID_W
    row = jnp.repeat(jnp.arange(rows), GRID_W).astype(jnp.float32)
    col = jnp.tile(jnp.arange(GRID_W), rows).astype(jnp.float32)
    n_freq = HEAD_DIM // 4
    inv = ROPE_BASE ** (-jnp.arange(n_freq, dtype=jnp.float32) / n_freq)
    ar, ac = row[:, None] * inv, col[:, None] * inv
    ang = jnp.concatenate([ar, ar, ac, ac], axis=-1)
    return jnp.cos(ang), jnp.sin(ang)


def apply_rope(x, cos, sin):
    x1, x2, x3, x4 = jnp.split(x, 4, axis=-1)
    rot = jnp.concatenate([-x2, x1, -x4, x3], axis=-1)
    out = x.astype(jnp.float32) * cos[:, None, :] + rot.astype(jnp.float32) * sin[:, None, :]
    return out.astype(x.dtype)


def sink_softmax(s, sink):
    m = jnp.maximum(jnp.max(s, axis=-1, keepdims=True), sink)
    p = jnp.exp(s - m)
    return p / (jnp.sum(p, axis=-1, keepdims=True) + jnp.exp(sink - m))


def windowed_gqa_with_sink(h_lat, h_ctx, w_qkv, w_o, sink, with_ctx_queries):
    B, L, _ = h_lat.shape
    C = h_ctx.shape[1]
    nb = L // ATTN_BLOCK
    qd, kvd = N_Q_HEADS * HEAD_DIM, N_KV_HEADS * HEAD_DIM
    scale = HEAD_DIM ** -0.5
    sink = sink.astype(jnp.float32).reshape(N_KV_HEADS, GQA_GROUP, 1, 1)

    q_l, k_l, v_l = jnp.split(h_lat @ w_qkv, [qd, qd + kvd], axis=-1)
    cos, sin = axial_rope_tables(L)
    q_l = apply_rope(q_l.reshape(B, L, N_Q_HEADS, HEAD_DIM), cos, sin)
    k_l = apply_rope(k_l.reshape(B, L, N_KV_HEADS, HEAD_DIM), cos, sin)
    v_l = v_l.reshape(B, L, N_KV_HEADS, HEAD_DIM)

    k_c, v_c = jnp.split(h_ctx @ w_qkv[:, qd:], 2, axis=-1)
    k_c = k_c.reshape(B, C, N_KV_HEADS, HEAD_DIM)
    v_c = v_c.reshape(B, C, N_KV_HEADS, HEAD_DIM)

    qb = q_l.reshape(B, nb, ATTN_BLOCK, N_KV_HEADS, GQA_GROUP, HEAD_DIM)

    def band(t):
        tp = jnp.pad(t, ((0, 0), (ATTN_BLOCK, ATTN_BLOCK), (0, 0), (0, 0)))
        tp = tp.reshape(B, nb + 2, ATTN_BLOCK, N_KV_HEADS, HEAD_DIM)
        return jnp.concatenate([tp[:, :-2], tp[:, 1:-1], tp[:, 2:]], axis=2)

    kw, vw = band(k_l), band(v_l)
    s_win = jnp.einsum('bnqhgd,bnkhd->bnhgqk', qb, kw, preferred_element_type=jnp.float32) * scale
    a = jnp.arange(ATTN_BLOCK)[:, None]
    kb = jnp.arange(3 * ATTN_BLOCK)[None, :]
    near = jnp.abs(kb - a - ATTN_BLOCK) <= WINDOW
    kpos = (jnp.arange(nb)[:, None, None] - 1) * ATTN_BLOCK + kb[None]
    valid = near[None] & (kpos >= 0) & (kpos < L)
    s_win = jnp.where(valid[None, :, None, None], s_win, NEG_INF)
    s_ctx = jnp.einsum('bnqhgd,bkhd->bnhgqk', qb, k_c, preferred_element_type=jnp.float32) * scale

    p = sink_softmax(jnp.concatenate([s_win, s_ctx], axis=-1), sink).astype(v_l.dtype)
    p_win, p_ctx = p[..., :3 * ATTN_BLOCK], p[..., 3 * ATTN_BLOCK:]
    o = (jnp.einsum('bnhgqk,bnkhd->bnqhgd', p_win, vw)
         + jnp.einsum('bnhgqk,bkhd->bnqhgd', p_ctx, v_c))
    y_lat = o.reshape(B, L, qd) @ w_o

    y_ctx = None
    if with_ctx_queries:
        q_c = (h_ctx @ w_qkv[:, :qd]).reshape(B, C, N_KV_HEADS, GQA_GROUP, HEAD_DIM)
        s_cc = jnp.einsum('bqhgd,bkhd->bhgqk', q_c, k_c, preferred_element_type=jnp.float32) * scale
        p_cc = sink_softmax(s_cc, sink).astype(v_c.dtype)
        y_ctx = jnp.einsum('bhgqk,bkhd->bqhgd', p_cc, v_c).reshape(B, C, qd) @ w_o
    return y_lat, y_ctx


def centred_short_conv(x, w, b):
    L = x.shape[1]
    r = HY_SHORT_W // 2
    xp = jnp.pad(x, ((0, 0), (r, r), (0, 0)))
    out = b
    for j in range(HY_SHORT_W):
        out = out + xp[:, j:j + L] * w[j]
    return out


def hyena_filter_spectra(L, f_w1, f_b1, f_w2, f_b2, f_w3, decay):
    f32 = jnp.float32
    t = jnp.arange(L, dtype=f32)
    t01 = t / L
    bands = jnp.arange(1, HY_EMB_BANDS + 1, dtype=f32)
    ang = 2.0 * math.pi * t[:, None] * bands[None, :] / L
    z = jnp.concatenate([t01[:, None], jnp.cos(ang), jnp.sin(ang)], axis=-1)
    h = jnp.sin(z @ f_w1.astype(f32) + f_b1.astype(f32))
    for i in range(HY_INNER_MLPS):
        h = jnp.sin(h @ f_w2[i].astype(f32) + f_b2[i].astype(f32))
    h = (h @ f_w3.astype(f32)).reshape(L, HY_ORDER, HY_N_DIR, D_MODEL)
    window = jnp.exp(-t01[:, None, None] * jnp.abs(decay.astype(f32))[None]) + HY_MOD_SHIFT
    h = h * window[:, :, None, :]
    fwd, bwd = h[:, :, 0], h[:, :, 1]
    filt2l = jnp.concatenate([fwd, jnp.zeros((1, HY_ORDER, D_MODEL), f32), bwd[1:][::-1]], axis=0)
    return jnp.fft.rfft(filt2l, axis=0)


def long_conv(z, spec, bias):
    L = z.shape[1]
    zf = z.astype(jnp.float32)
    y = jnp.fft.irfft(jnp.fft.rfft(zf, n=2 * L, axis=1) * spec[None], n=2 * L, axis=1)[:, :L]
    return (y + zf * bias.astype(jnp.float32)).astype(z.dtype)


def hyena_sequence(h, w_in, conv_w, conv_b, f_w1, f_b1, f_w2, f_b2, f_w3, decay, fbias, w_out):
    L = h.shape[1]
    u = centred_short_conv(h @ w_in, conv_w, conv_b)
    v, x1, x2 = jnp.split(u, 3, axis=-1)
    spec = hyena_filter_spectra(L, f_w1, f_b1, f_w2, f_b2, f_w3, decay)
    z = x1 * long_conv(v, spec[:, 0], fbias[0])
    y = x2 * long_conv(z, spec[:, 1], fbias[1])
    return y @ w_out


def peer_ffn(h, w_q, keys1, keys2, u_tab, v_tab):
    B, L, D = h.shape
    chunks = h.reshape((B * L) // PEER_CHUNK, PEER_CHUNK, D)

    def one_chunk(xc):
        q = (xc @ w_q).reshape(PEER_CHUNK, PEER_HEADS, PEER_D_KEY)
        q1, q2 = q[..., :PEER_HALF], q[..., PEER_HALF:]
        s1 = jnp.einsum('chd,hkd->chk', q1, keys1, preferred_element_type=jnp.float32)
        s2 = jnp.einsum('chd,hkd->chk', q2, keys2, preferred_element_type=jnp.float32)
        v1, i1 = lax.top_k(s1, PEER_TOPK)
        v2, i2 = lax.top_k(s2, PEER_TOPK)
        cand_s = (v1[..., :, None] + v2[..., None, :]).reshape(PEER_CHUNK, PEER_HEADS, PEER_TOPK * PEER_TOPK)
        cand_i = (i1[..., :, None] * PEER_N_KEYS + i2[..., None, :]).reshape(PEER_CHUNK, PEER_HEADS, PEER_TOPK * PEER_TOPK)
        top_s, pos = lax.top_k(cand_s, PEER_TOPK)
        idx = jnp.take_along_axis(cand_i, pos, axis=-1)
        g = jax.nn.softmax(top_s, axis=-1)
        u = jnp.take(u_tab, idx, axis=0)
        act = jax.nn.gelu(jnp.einsum('chkd,cd->chk', u, xc, preferred_element_type=jnp.float32), approximate=False)
        v = jnp.take(v_tab, idx, axis=0)
        return jnp.einsum('chk,chkd->cd', (g * act).astype(xc.dtype), v)

    return lax.map(one_chunk, chunks).reshape(B, L, D)


def setup_inputs(seed: int = 0) -> dict:
    key = jax.random.key(seed)
    ks = jax.random.split(key, 32)
    f32 = jnp.float32
    D = D_MODEL
    qkv_w = N_Q_HEADS * HEAD_DIM + 2 * N_KV_HEADS * HEAD_DIM

    def nrm(k, shape, s):
        return jax.random.normal(k, shape, f32) * s

    decay_base = jnp.abs(jnp.linspace(math.log(1e-2) / 0.3, math.log(1e-2) / 1.5, D, dtype=f32))
    return {
        "x": nrm(ks[0], (BATCH, SEQ, D), 1.0),
        "c": nrm(ks[1], (BATCH, D), 1.0),
        "ctx": nrm(ks[2], (BATCH, CTX_LEN, D), 1.0),
        "c_ctx": nrm(ks[3], (D,), 1.0),
        "ada_w": nrm(ks[4], (DEPTH, D, N_MODS * D), 0.5 * D ** -0.5),
        "ada_b": nrm(ks[5], (DEPTH, N_MODS * D), 0.02),
        "norm_mix_g": 1.0 + nrm(ks[6], (DEPTH, D), 0.02),
        "norm_ffn_g": 1.0 + nrm(ks[7], (DEPTH, D), 0.02),
        "final_g": 1.0 + nrm(ks[8], (D,), 0.02),
        "attn_w_qkv": nrm(ks[9], (N_ATTN_LAYERS, D, qkv_w), D ** -0.5),
        "attn_w_o": nrm(ks[10], (N_ATTN_LAYERS, N_Q_HEADS * HEAD_DIM, D), (N_Q_HEADS * HEAD_DIM) ** -0.5),
        "attn_sink": nrm(ks[11], (N_ATTN_LAYERS, N_Q_HEADS), 1.0),
        "hy_w_in": nrm(ks[12], (N_HYENA_LAYERS, D, 3 * D), D ** -0.5),
        "hy_conv_w": nrm(ks[13], (N_HYENA_LAYERS, HY_SHORT_W, 3 * D), HY_SHORT_W ** -0.5),
        "hy_conv_b": nrm(ks[14], (N_HYENA_LAYERS, 3 * D), 0.02),
        "hy_f_w1": nrm(ks[15], (N_HYENA_LAYERS, HY_EMB_DIM, HY_FILTER_HIDDEN), HY_EMB_DIM ** -0.5),
        "hy_f_b1": nrm(ks[16], (N_HYENA_LAYERS, HY_FILTER_HIDDEN), 0.1),
        "hy_f_w2": nrm(ks[17], (N_HYENA_LAYERS, HY_INNER_MLPS, HY_FILTER_HIDDEN, HY_FILTER_HIDDEN), HY_FILTER_HIDDEN ** -0.5),
        "hy_f_b2": nrm(ks[18], (N_HYENA_LAYERS, HY_INNER_MLPS, HY_FILTER_HIDDEN), 0.1),
        "hy_f_w3": nrm(ks[19], (N_HYENA_LAYERS, HY_FILTER_HIDDEN, HY_ORDER * HY_N_DIR * D), 0.01),
        "hy_decay": decay_base[None, None, :] * (1.0 + nrm(ks[20], (N_HYENA_LAYERS, HY_ORDER, D), 0.05)),
        "hy_fbias": nrm(ks[21], (N_HYENA_LAYERS, HY_ORDER, D), 0.1),
        "hy_w_out": nrm(ks[22], (N_HYENA_LAYERS, D, D), D ** -0.5),
        "peer_w_q": nrm(ks[23], (DEPTH, D, PEER_HEADS * PEER_D_KEY), D ** -0.5),
        "peer_keys1": nrm(ks[24], (DEPTH, PEER_HEADS, PEER_N_KEYS, PEER_HALF), PEER_HALF ** -0.5),
        "peer_keys2": nrm(ks[25], (DEPTH, PEER_HEADS, PEER_N_KEYS, PEER_HALF), PEER_HALF ** -0.5),
        "peer_u": nrm(ks[26], (DEPTH, PEER_N_EXPERTS, D), D ** -0.5),
        "peer_v": nrm(ks[27], (DEPTH, PEER_N_EXPERTS, D), 0.25),
    }


def reference(x, c, ctx, c_ctx, ada_w, ada_b, norm_mix_g, norm_ffn_g, final_g,
              attn_w_qkv, attn_w_o, attn_sink,
              hy_w_in, hy_conv_w, hy_conv_b, hy_f_w1, hy_f_b1, hy_f_w2, hy_f_b2, hy_f_w3,
              hy_decay, hy_fbias, hy_w_out,
              peer_w_q, peer_keys1, peer_keys2, peer_u, peer_v):
    D = D_MODEL
    last_attn = max(i for i in range(DEPTH) if i % N_MIXERS == 0)
    cond_lat = jax.nn.silu(c)[:, None, :]
    cond_ctx = jax.nn.silu(c_ctx)[None, None, :]
    h_lat, h_ctx = x, ctx
    for i in range(DEPTH):
        is_attn = (i % N_MIXERS) == 0
        j = i // N_MIXERS
        ctx_update = i < last_attn
        sh1, sc1, g1, sh2, sc2, g2 = jnp.split(cond_lat @ ada_w[i] + ada_b[i], N_MODS, axis=-1)
        a_lat = modulate(h_lat, norm_mix_g[i], sh1, sc1)
        if ctx_update:
            csh1, csc1, cg1, csh2, csc2, cg2 = jnp.split(cond_ctx @ ada_w[i] + ada_b[i], N_MODS, axis=-1)
        else:
            csh1, csc1 = jnp.split(cond_ctx @ ada_w[i][:, :2 * D] + ada_b[i][:2 * D], 2, axis=-1)

        if is_attn:
            a_ctx = modulate(h_ctx, norm_mix_g[i], csh1, csc1)
            y_lat, y_ctx = windowed_gqa_with_sink(a_lat, a_ctx, attn_w_qkv[j], attn_w_o[j],
                                                  attn_sink[j], ctx_update)
        else:
            hy_args = (hy_w_in[j], hy_conv_w[j], hy_conv_b[j], hy_f_w1[j], hy_f_b1[j], hy_f_w2[j],
                       hy_f_b2[j], hy_f_w3[j], hy_decay[j], hy_fbias[j], hy_w_out[j])
            y_lat = hyena_sequence(a_lat, *hy_args)
            y_ctx = None
            if ctx_update:
                a_ctx = modulate(h_ctx, norm_mix_g[i], csh1, csc1)
                y_ctx = hyena_sequence(a_ctx, *hy_args)

        peer_args = (peer_w_q[i], peer_keys1[i], peer_keys2[i], peer_u[i], peer_v[i])
        h_lat = h_lat + g1 * y_lat
        h_lat = h_lat + g2 * peer_ffn(modulate(h_lat, norm_ffn_g[i], sh2, sc2), *peer_args)
        if ctx_update:
            h_ctx = h_ctx + cg1 * y_ctx
            h_ctx = h_ctx + cg2 * peer_ffn(modulate(h_ctx, norm_ffn_g[i], csh2, csc2), *peer_args)
    return rms_norm(h_lat, final_g)
```

```python
import functools
import math

import numpy as np
import jax
import jax.numpy as jnp
from jax import lax
from jax.experimental import pallas as pl
from jax.experimental.pallas import tpu as pltpu

F32 = jnp.float32
BF16 = jnp.bfloat16

LANES = 128
HEAD_DIM = 128
GRID_W = 64
ATTN_BLOCK = 128
ROPE_BASE = 10000.0
RMS_EPS = 1e-6
NEG_INF = -1e30
N_MODS = 6
HY_EMB_BANDS = 16
HY_MOD_SHIFT = 0.05
PEER_TOPK = 16
VMEM_LIMIT = 56 * 2 ** 20
NT_DIMS = (((1,), (1,)), ((), ()))


def _params(*sem):
    return pltpu.CompilerParams(dimension_semantics=sem, vmem_limit_bytes=VMEM_LIMIT)


def _tile(n, pref):
    if n <= pref:
        return n
    t = (pref // LANES) * LANES
    while n % t:
        t -= LANES
    return t


def _modulate_body(h_ref, g_ref, sh_ref, sc_ref, o_ref):
    x = h_ref[...]
    ms = jnp.mean(x * x, axis=-1, keepdims=True)
    y = x * lax.rsqrt(ms + RMS_EPS) * g_ref[...]
    o_ref[...] = (y * (1.0 + sc_ref[...]) + sh_ref[...]).astype(o_ref.dtype)


def modulate(h, g, shift, scale, out_dtype=None):
    B, L, D = h.shape
    out_dtype = out_dtype or BF16
    tm = _tile(L, 512)
    vec = pl.BlockSpec((None, 1, D), lambda b, i: (b, 0, 0))
    return pl.pallas_call(
        _modulate_body,
        grid=(B, L // tm),
        in_specs=[pl.BlockSpec((None, tm, D), lambda b, i: (b, i, 0)),
                  pl.BlockSpec((1, D), lambda b, i: (0, 0)), vec, vec],
        out_specs=pl.BlockSpec((None, tm, D), lambda b, i: (b, i, 0)),
        out_shape=jax.ShapeDtypeStruct((B, L, D), out_dtype),
        compiler_params=_params("parallel", "parallel"),
        name="modulate",
    )(h, g.reshape(1, D), shift, scale)


def _mm_plain_body(a_ref, w_ref, o_ref):
    acc = jnp.dot(a_ref[...], w_ref[...].astype(BF16), preferred_element_type=F32)
    o_ref[...] = acc.astype(o_ref.dtype)


def _mm_res_body(a_ref, w_ref, r_ref, g_ref, o_ref):
    acc = jnp.dot(a_ref[...], w_ref[...].astype(BF16), preferred_element_type=F32)
    o_ref[...] = r_ref[...] + g_ref[...] * acc


def _mm_rope_body(a_ref, w_ref, cos_ref, sa_ref, sb_ref, o_ref, *, n_rope):
    acc = jnp.dot(a_ref[...], w_ref[...].astype(BF16), preferred_element_type=F32)
    j = pl.program_id(2)

    @pl.when(j < n_rope)
    def _():
        cos, sa, sb = cos_ref[...], sa_ref[...], sb_ref[...]
        quarter = HEAD_DIM // 4
        for s in range(acc.shape[1] // HEAD_DIM):
            x = acc[:, s * HEAD_DIM:(s + 1) * HEAD_DIM]
            y = (x * cos + pltpu.roll(x, quarter, axis=1) * sa
                 + pltpu.roll(x, HEAD_DIM - quarter, axis=1) * sb)
            o_ref[:, s * HEAD_DIM:(s + 1) * HEAD_DIM] = y.astype(o_ref.dtype)

    @pl.when(j >= n_rope)
    def _():
        o_ref[...] = acc.astype(o_ref.dtype)


def _mm_conv3_body(a_ref, w_ref, cw_ref, cb_ref, o_ref):
    y = jnp.dot(a_ref[...], w_ref[...].astype(BF16), preferred_element_type=F32)
    L = y.shape[0]
    row = lax.broadcasted_iota(jnp.int32, y.shape, 0)
    prev = jnp.where(row == 0, 0.0, pltpu.roll(y, 1, axis=0))
    nxt = jnp.where(row == L - 1, 0.0, pltpu.roll(y, L - 1, axis=0))
    o_ref[...] = cb_ref[...] + prev * cw_ref[0:1, :] + y * cw_ref[1:2, :] + nxt * cw_ref[2:3, :]


def matmul(a, w, *, mode="plain", out_dtype=F32, tm=1024, tn=512, res=None, gate=None,
           rope=None, n_rope_cols=0, conv_w=None, conv_b=None):
    B, L, K = a.shape
    N = w.shape[1]
    tm = L if mode == "conv3" else _tile(L, tm)
    if mode == "rope":
        tn = _tile(math.gcd(N, n_rope_cols), tn)
    else:
        tn = _tile(N, 256 if mode == "conv3" else tn)
    grid = (B, L // tm, N // tn)
    a_spec = pl.BlockSpec((None, tm, K), lambda b, i, j: (b, i, 0))
    w_spec = pl.BlockSpec((K, tn), lambda b, i, j: (0, j))
    o_spec = pl.BlockSpec((None, tm, tn), lambda b, i, j: (b, i, j))
    in_specs, args = [a_spec, w_spec], [a, w]
    if mode == "plain":
        body = _mm_plain_body
    elif mode == "res":
        body, out_dtype = _mm_res_body, F32
        in_specs += [o_spec, pl.BlockSpec((None, 1, tn), lambda b, i, j: (b, 0, j))]
        args += [res, gate]
    elif mode == "rope":
        assert n_rope_cols % tn == 0
        body = functools.partial(_mm_rope_body, n_rope=n_rope_cols // tn)
        tab = pl.BlockSpec((tm, HEAD_DIM), lambda b, i, j: (i, 0))
        in_specs += [tab, tab, tab]
        args += list(rope)
    elif mode == "conv3":
        body, out_dtype = _mm_conv3_body, F32
        in_specs += [pl.BlockSpec((3, tn), lambda b, i, j: (0, j)),
                     pl.BlockSpec((1, tn), lambda b, i, j: (0, j))]
        args += [conv_w, conv_b.reshape(1, N)]
    else:
        raise ValueError(mode)
    return pl.pallas_call(
        body, grid=grid, in_specs=in_specs, out_specs=o_spec,
        out_shape=jax.ShapeDtypeStruct((B, L, N), out_dtype),
        compiler_params=_params("parallel", "parallel", "arbitrary"),
        name="mm_" + mode,
    )(*args)


def _attn_body(sink_ref, q_ref, *refs, nkv, grp, nb, windowed):
    if windowed:
        kp_ref, kc_ref, kn_ref, kx_ref, o_ref = refs
    else:
        kx_ref, o_ref = refs
    n = pl.program_id(1)
    scale = HEAD_DIM ** -0.5
    rows = grp * ATTN_BLOCK
    a = lax.broadcasted_iota(jnp.int32, (rows, ATTN_BLOCK), 0) % ATTN_BLOCK
    kb = lax.broadcasted_iota(jnp.int32, (rows, ATTN_BLOCK), 1)

    def head_cols(ref, c):
        return ref[:, c * HEAD_DIM:(c + 1) * HEAD_DIM]

    for h in range(nkv):
        qh = jnp.concatenate([head_cols(q_ref, h * grp + g) for g in range(grp)], axis=0)
        sink = jnp.concatenate(
            [jnp.full((ATTN_BLOCK, 1), sink_ref[h * grp + g], F32) for g in range(grp)], axis=0)

        def scores(ref):
            return lax.dot_general(qh, head_cols(ref, h), NT_DIMS, preferred_element_type=F32) * scale

        pieces = []
        if windowed:
            s_p = jnp.where(jnp.logical_and(kb >= a, n > 0), scores(kp_ref), NEG_INF)
            s_n = jnp.where(jnp.logical_and(kb <= a, n < nb - 1), scores(kn_ref), NEG_INF)
            pieces += [(s_p, kp_ref), (scores(kc_ref), kc_ref), (s_n, kn_ref)]
        pieces.append((scores(kx_ref), kx_ref))

        m = sink
        for s, _ in pieces:
            m = jnp.maximum(m, jnp.max(s, axis=-1, keepdims=True))
        den = jnp.exp(sink - m)
        o = jnp.zeros((rows, HEAD_DIM), F32)
        for s, ref in pieces:
            p = jnp.exp(s - m)
            den = den + jnp.sum(p, axis=-1, keepdims=True)
            o = o + jnp.dot(p.astype(BF16), head_cols(ref, nkv + h), preferred_element_type=F32)
        o = o / den
        for g in range(grp):
            c = h * grp + g
            o_ref[:, c * HEAD_DIM:(c + 1) * HEAD_DIM] = (
                o[g * ATTN_BLOCK:(g + 1) * ATTN_BLOCK].astype(o_ref.dtype))


def attention(qkv, qkv_ctx, sink, *, nq, nkv, windowed):
    B, L, _ = qkv.shape
    C = qkv_ctx.shape[1]
    nb = L // ATTN_BLOCK
    qd, kvd = nq * HEAD_DIM, nkv * HEAD_DIM
    assert qd % (2 * kvd) == 0
    kv_col = qd // (2 * kvd)
    q_spec = pl.BlockSpec((None, ATTN_BLOCK, qd), lambda b, n, s: (b, n, 0))

    def kv_spec(off):
        return pl.BlockSpec(
            (None, ATTN_BLOCK, 2 * kvd),
            lambda b, n, s: (b, jnp.clip(n + off, 0, nb - 1), kv_col))

    ctx_spec = pl.BlockSpec((None, C, 2 * kvd), lambda b, n, s: (b, 0, kv_col))
    if windowed:
        in_specs = [q_spec, kv_spec(-1), kv_spec(0), kv_spec(1), ctx_spec]
        args = [qkv, qkv, qkv, qkv, qkv_ctx]
    else:
        in_specs, args = [q_spec, ctx_spec], [qkv, qkv_ctx]
    body = functools.partial(_attn_body, nkv=nkv, grp=nq // nkv, nb=nb, windowed=windowed)
    return pl.pallas_call(
        body,
        grid_spec=pltpu.PrefetchScalarGridSpec(
            num_scalar_prefetch=1, grid=(B, nb), in_specs=in_specs,
            out_specs=pl.BlockSpec((None, ATTN_BLOCK, qd), lambda b, n, s: (b, n, 0))),
        out_shape=jax.ShapeDtypeStruct((B, L, qd), BF16),
        compiler_params=_params("parallel", "parallel"),
        name="attn_win" if windowed else "attn_ctx",
    )(sink.astype(F32), *args)


def rope_tables(L):
    rows = L // GRID_W
    row = jnp.repeat(jnp.arange(rows), GRID_W).astype(F32)
    col = jnp.tile(jnp.arange(GRID_W), rows).astype(F32)
    n_freq = HEAD_DIM // 4
    inv = ROPE_BASE ** (-jnp.arange(n_freq, dtype=F32) / n_freq)
    ar, ac = row[:, None] * inv, col[:, None] * inv
    ang = jnp.concatenate([ar, ar, ac, ac], axis=-1)
    cos, sin = jnp.cos(ang), jnp.sin(ang)
    odd = (jnp.arange(HEAD_DIM) // n_freq) % 2 == 1
    return cos, jnp.where(odd, sin, 0.0), jnp.where(odd, 0.0, -sin)


@functools.lru_cache(maxsize=None)
def _dft_matrices(L):
    N = 2 * L
    fb = min(256, L)
    nfb = L // fb
    k = np.arange(L, dtype=np.int64)[:, None]
    t = np.arange(L, dtype=np.int64)[None, :]
    ang = 2.0 * np.pi * ((k * t) % N).astype(np.float64) / N
    c, s = np.cos(ang), np.sin(ang)
    nyq = np.where(np.arange(L) % 2 == 0, 1.0, -1.0)
    f_re, f_im = c.copy(), -s
    f_im[0] = nyq
    i_re, i_im = 2.0 * c / N, -2.0 * s / N
    i_re[0] = 1.0 / N
    i_im[0] = nyq / N
    fwd = np.concatenate([f_re.reshape(nfb, fb, L), f_im.reshape(nfb, fb, L)], axis=1)
    inv = np.concatenate([i_re.reshape(nfb, fb, L), i_im.reshape(nfb, fb, L)], axis=1)
    inv = np.transpose(inv, (0, 2, 1))
    return fwd.astype(np.float32), inv.astype(np.float32), fb


def _filter_body(z_ref, w1_ref, b1_ref, w2_ref, b2_ref, w3_ref, dec_ref, o_ref, *, n_inner):
    hi = lax.Precision.HIGHEST
    h = jnp.sin(jnp.dot(z_ref[...], w1_ref[...], precision=hi, preferred_element_type=F32)
                + b1_ref[...])
    for i in range(n_inner):
        h = jnp.sin(jnp.dot(h, w2_ref[i], precision=hi, preferred_element_type=F32)
                    + b2_ref[i])
    y = jnp.dot(h, w3_ref[...], precision=hi, preferred_element_type=F32)
    L = y.shape[0]
    t01 = lax.broadcasted_iota(jnp.int32, y.shape, 0).astype(F32) / L
    o_ref[...] = y * (jnp.exp(-t01 * jnp.abs(dec_ref[...])) + HY_MOD_SHIFT)


def hyena_filters(L, f_w1, f_b1, f_w2, f_b2, f_w3, decay):
    order, D = decay.shape
    hid = f_w1.shape[1]
    n_inner = f_w2.shape[0]
    n_dir = f_w3.shape[1] // (order * D)
    t = np.arange(L, dtype=np.float32)
    bands = np.arange(1, HY_EMB_BANDS + 1, dtype=np.float32)
    ang = (np.float32(2.0 * math.pi) * t[:, None] * bands[None, :] / np.float32(L)).astype(np.float32)
    z = np.concatenate([(t / np.float32(L))[:, None], np.cos(ang), np.sin(ang)], axis=-1)
    emb = z.shape[1]
    emb_pad = -(-emb // 8) * 8
    z = np.pad(z, ((0, 0), (0, emb_pad - emb))).astype(np.float32)
    w1 = jnp.pad(f_w1, ((0, emb_pad - emb), (0, 0)))
    tn = _tile(D, 1024)
    per_o = n_dir * D // tn
    full = lambda shape: pl.BlockSpec(shape, lambda j: (0,) * len(shape))
    out = pl.pallas_call(
        functools.partial(_filter_body, n_inner=n_inner),
        grid=(order * n_dir * D // tn,),
        in_specs=[full((L, emb_pad)), full((emb_pad, hid)), full((1, hid)),
                  full((n_inner, hid, hid)), full((n_inner, 1, hid)),
                  pl.BlockSpec((hid, tn), lambda j: (0, j)),
                  pl.BlockSpec((None, 1, tn), lambda j: (j // per_o, 0, j % (D // tn)))],
        out_specs=pl.BlockSpec((L, tn), lambda j: (0, j)),
        out_shape=jax.ShapeDtypeStruct((L, order * n_dir * D), F32),
        compiler_params=_params("parallel"),
        name="hyena_filter",
    )(jnp.asarray(z), w1, f_b1.reshape(1, hid), f_w2, f_b2.reshape(n_inner, 1, hid), f_w3,
      decay.reshape(order, 1, D))
    return out.reshape(L, order, n_dir, D)


def hyena_spectra(L, fwd, filt):
    order, D = filt.shape[1], filt.shape[3]
    nfb, fb2, _ = fwd.shape
    fb = fb2 // 2
    f = filt[:, :, 0]
    g = filt[:, :, 1].at[0].set(0.0)
    cols = jnp.stack([f + g, f - g], axis=2).reshape(L, order * 2 * D)
    hi = cols.astype(BF16)
    lo = (cols - hi.astype(F32)).astype(BF16)
    spec = matmul(fwd.reshape(1, nfb * fb2, L), jnp.concatenate([hi, lo], axis=1))
    spec = spec.reshape(nfb, fb2, 2, order, 2, D)
    spec = spec[:, :, 0] + spec[:, :, 1]
    planes = []
    for o in range(order):
        a = spec[:, :fb, o, 0]
        b = spec[:, fb:, o, 1].at[0, 0].set(0.0)
        c = a.at[0, 0].set(spec[0, fb, o, 0])
        planes.append((a, b, c))
    return planes


def _lconv_body(v_ref, x_ref, fw_ref, iv_ref, ha_ref, hb_ref, hc_ref, bias_ref, o_ref,
                acc_ref, vb_ref, *, fb):
    f = pl.program_id(2)

    @pl.when(f == 0)
    def _():
        vb_ref[...] = v_ref[...].astype(BF16)
        acc_ref[...] = jnp.zeros_like(acc_ref)

    spec = jnp.dot(fw_ref[...], vb_ref[...], preferred_element_type=F32)
    xr, xi = spec[:fb], spec[fb:]
    a, b, c = ha_ref[...], hb_ref[...], hc_ref[...]
    yr = (xr * a - xi * b).astype(BF16)
    yi = (xr * b + xi * c).astype(BF16)
    acc_ref[...] += (jnp.dot(iv_ref[:, :fb], yr, preferred_element_type=F32)
                     + jnp.dot(iv_ref[:, fb:], yi, preferred_element_type=F32))

    @pl.when(f == pl.num_programs(2) - 1)
    def _():
        o_ref[...] = (x_ref[...] * (acc_ref[...] + v_ref[...] * bias_ref[...])).astype(o_ref.dtype)


def long_conv_gated(src, src_blk, gate, gate_blk, fwd, inv, planes, bias, out_dtype):
    B, L, _ = src.shape
    D = bias.shape[0]
    nfb, fb2, _ = fwd.shape
    fb = fb2 // 2
    tc = _tile(D, 512)
    ncb = D // tc
    ha, hb, hc = planes
    col = lambda blk: pl.BlockSpec((None, L, tc), lambda c, b, f: (b, 0, blk * ncb + c))
    hspec = pl.BlockSpec((None, fb, tc), lambda c, b, f: (f, 0, c))
    return pl.pallas_call(
        functools.partial(_lconv_body, fb=fb),
        grid=(ncb, B, nfb),
        in_specs=[col(src_blk), col(gate_blk),
                  pl.BlockSpec((None, fb2, L), lambda c, b, f: (f, 0, 0)),
                  pl.BlockSpec((None, L, fb2), lambda c, b, f: (f, 0, 0)),
                  hspec, hspec, hspec,
                  pl.BlockSpec((1, tc), lambda c, b, f: (0, c))],
        out_specs=pl.BlockSpec((None, L, tc), lambda c, b, f: (b, 0, c)),
        out_shape=jax.ShapeDtypeStruct((B, L, D), out_dtype),
        scratch_shapes=[pltpu.VMEM((L, tc), F32), pltpu.VMEM((L, tc), BF16)],
        compiler_params=_params("parallel", "parallel", "arbitrary"),
        name="long_conv",
    )(src, gate, fwd, inv, ha, hb, hc, bias.reshape(1, D))


def hyena_mix(a, res, gate, w_in, conv_w, conv_b, w_out, fbias, dft, planes):
    fwd, inv = dft
    u = matmul(a, w_in, mode="conv3", conv_w=conv_w, conv_b=conv_b)
    z = long_conv_gated(u, 0, u, 1, fwd, inv, planes[0], fbias[0], F32)
    y = long_conv_gated(z, 0, u, 2, fwd, inv, planes[1], fbias[1], BF16)
    return matmul(y, w_out, mode="res", res=res, gate=gate)


def _topk_rows(s, k):
    n = s.shape[0]
    row = lax.broadcasted_iota(jnp.int32, s.shape, 0)
    vals, idxs = [], []
    for _ in range(k):
        m = jnp.max(s, axis=0, keepdims=True)
        i = jnp.min(jnp.where(s == m, row, n), axis=0, keepdims=True)
        vals.append(m)
        idxs.append(i)
        s = jnp.where(row == i, -jnp.inf, s)
    return vals, idxs


def _route_body(q_ref, k1_ref, k2_ref, idx_ref, g_ref, *, half, n_keys):
    T = q_ref.shape[0]
    for c in range(T // LANES):
        q = q_ref[c * LANES:(c + 1) * LANES, :]
        s1 = lax.dot_general(k1_ref[...], q[:, :half], NT_DIMS, preferred_element_type=F32)
        s2 = lax.dot_general(k2_ref[...], q[:, half:], NT_DIMS, preferred_element_type=F32)
        v1, i1 = _topk_rows(s1, PEER_TOPK)
        v2, i2 = _topk_rows(s2, PEER_TOPK)
        v2c = jnp.concatenate(v2, axis=0)
        i2c = jnp.concatenate(i2, axis=0)
        cand = jnp.concatenate([v1[i] + v2c for i in range(PEER_TOPK)], axis=0)
        cid = jnp.concatenate([i1[i] * n_keys + i2c for i in range(PEER_TOPK)], axis=0)
        top, pos = _topk_rows(cand, PEER_TOPK)
        row = lax.broadcasted_iota(jnp.int32, cand.shape, 0)
        ids = [jnp.max(jnp.where(row == p, cid, -1), axis=0, keepdims=True) for p in pos]
        e = [jnp.exp(t - top[0]) for t in top]
        den = e[0]
        for x in e[1:]:
            den = den + x
        idx_ref[:, c * LANES:(c + 1) * LANES] = jnp.concatenate(ids, axis=0)
        g_ref[:, c * LANES:(c + 1) * LANES] = jnp.concatenate(e, axis=0) / den


def peer_route(q, keys1, keys2):
    M = q.shape[0]
    H, n_keys, half = keys1.shape
    T = _tile(M, 256)
    kspec = pl.BlockSpec((None, n_keys, half), lambda i, h: (h, 0, 0))
    ospec = pl.BlockSpec((None, PEER_TOPK, T), lambda i, h: (h, 0, i))
    return pl.pallas_call(
        functools.partial(_route_body, half=half, n_keys=n_keys),
        grid=(M // T, H),
        in_specs=[pl.BlockSpec((T, 2 * half), lambda i, h: (i, h)), kspec, kspec],
        out_specs=[ospec, ospec],
        out_shape=[jax.ShapeDtypeStruct((H, PEER_TOPK, M), jnp.int32),
                   jax.ShapeDtypeStruct((H, PEER_TOPK, M), F32)],
        compiler_params=_params("parallel", "arbitrary"),
        name="peer_route",
    )(q, keys1.astype(BF16), keys2.astype(BF16))


GATHER_SLOTS = 4
X_GROUP = 16


def _gather_body(idx_hbm, wg_ref, x_ref, res_ref, g2_ref, uv_hbm, o_ref,
                 idx_smem, rows, isem, sems, *, n_sel, d):
    tb = x_ref.shape[0]
    step = pl.program_id(0)
    icp = pltpu.make_async_copy(idx_hbm.at[step], idx_smem, isem)
    icp.start()
    icp.wait()

    def row_copy(e, j, slot):
        return pltpu.make_async_copy(uv_hbm.at[pl.ds(e, 1), :], rows.at[slot, pl.ds(j, 1), :],
                                     sems.at[slot])

    def issue(t, slot):
        for j in range(n_sel):
            row_copy(idx_smem[t * n_sel + j], j, slot).start()

    for p in range(GATHER_SLOTS - 1):
        issue(p, p)

    def token(t, carry):
        slot = t % GATHER_SLOTS
        ahead = t + GATHER_SLOTS - 1

        @pl.when(ahead < tb)
        def _():
            issue(ahead, ahead % GATHER_SLOTS)

        for j in range(n_sel):
            row_copy(0, j, slot).wait()
        r = rows[slot]
        xg = x_ref[pl.ds(pl.multiple_of((t // X_GROUP) * X_GROUP, X_GROUP), X_GROUP), :]
        actg = lax.dot_general(xg, r[:, :d].astype(BF16), NT_DIMS, preferred_element_type=F32)
        mine = lax.broadcasted_iota(jnp.int32, actg.shape, 0) == t % X_GROUP
        act = jnp.sum(jnp.where(mine, actg, 0.0), axis=0, keepdims=True)
        gelu = 0.5 * act * (1.0 + lax.erf(act * (2.0 ** -0.5)))
        w = jnp.broadcast_to(wg_ref[pl.ds(t, 1), :] * gelu, (8, n_sel))
        out = jnp.dot(w.astype(BF16), r[:, d:].astype(BF16), preferred_element_type=F32)
        o_ref[pl.ds(t, 1), :] = res_ref[pl.ds(t, 1), :] + g2_ref[...] * out[0:1]
        return carry

    lax.fori_loop(0, tb, token, 0)


def peer_experts(idx, gates, x, res, gate2, uv):
    B, L, D = x.shape
    M = B * L
    n_sel = idx.shape[1]
    tb = _tile(L, 128)
    lb = L // tb
    body = functools.partial(_gather_body, n_sel=n_sel, d=D)
    tok = lambda shape_last: pl.BlockSpec((tb, shape_last), lambda i: (i, 0))
    out = pl.pallas_call(
        body,
        grid=(M // tb,),
        in_specs=[pl.BlockSpec(memory_space=pl.ANY), tok(n_sel), tok(D), tok(D),
                  pl.BlockSpec((None, 1, D), lambda i: (i // lb, 0, 0)),
                  pl.BlockSpec(memory_space=pl.ANY)],
        out_specs=tok(D),
        out_shape=jax.ShapeDtypeStruct((M, D), F32),
        scratch_shapes=[pltpu.SMEM((tb * n_sel,), jnp.int32),
                        pltpu.VMEM((GATHER_SLOTS, n_sel, 2 * D), F32),
                        pltpu.SemaphoreType.DMA(()),
                        pltpu.SemaphoreType.DMA((GATHER_SLOTS,))],
        compiler_params=_params("arbitrary"),
        name="peer_experts",
    )(idx.reshape(M // tb, tb * n_sel), gates, x.reshape(M, D), res.reshape(M, D), gate2, uv)
    return out.reshape(B, L, D)


def peer_mix(h, g, shift, scale, gate2, w_q, keys1, keys2, uv):
    B, L, D = h.shape
    a = modulate(h, g, shift, scale)
    q = matmul(a, w_q, out_dtype=BF16)
    idx, gates = peer_route(q.reshape(B * L, -1), keys1, keys2)
    n_sel = idx.shape[0] * idx.shape[1]
    idx = idx.reshape(n_sel, B * L).T
    gates = gates.reshape(n_sel, B * L).T
    return peer_experts(idx, gates, a, h, gate2, uv)


def kernel(x, c, ctx, c_ctx, ada_w, ada_b, norm_mix_g, norm_ffn_g, final_g, attn_w_qkv, attn_w_o, attn_sink, hy_w_in, hy_conv_w, hy_conv_b, hy_f_w1, hy_f_b1, hy_f_w2, hy_f_b2, hy_f_w3, hy_decay, hy_fbias, hy_w_out, peer_w_q, peer_keys1, peer_keys2, peer_u, peer_v):
    B, L, D = x.shape
    C = ctx.shape[1]
    depth = ada_w.shape[0]
    n_mixers = 2
    last_attn = max(i for i in range(depth) if i % n_mixers == 0)
    nq = attn_w_o.shape[1] // HEAD_DIM
    nkv = (attn_w_qkv.shape[2] - nq * HEAD_DIM) // (2 * HEAD_DIM)

    cond = jnp.concatenate([jax.nn.silu(c), jax.nn.silu(c_ctx)[None]], axis=0)
    cond = jnp.pad(cond, ((0, -(B + 1) % 16), (0, 0))).astype(BF16)[None]
    rope = rope_tables(L)
    h_lat, h_ctx = x, ctx

    for i in range(depth):
        is_attn = i % n_mixers == 0
        j = i // n_mixers
        ctx_update = i < last_attn
        mods = matmul(cond, ada_w[i])[0, :B + 1] + ada_b[i]
        lat = [m[:, None, :] for m in jnp.split(mods[:B], N_MODS, axis=-1)]
        cx = [jnp.broadcast_to(m[None], (B, 1, D)) for m in jnp.split(mods[B:], N_MODS, axis=-1)]
        sh1, sc1, g1, sh2, sc2, g2 = lat
        csh1, csc1, cg1, csh2, csc2, cg2 = cx
        a_lat = modulate(h_lat, norm_mix_g[i], sh1, sc1)

        if is_attn:
            w_qkv = attn_w_qkv[j].astype(BF16)
            w_o = attn_w_o[j].astype(BF16)
            a_ctx = modulate(h_ctx, norm_mix_g[i], csh1, csc1)
            qkv = matmul(a_lat, w_qkv, mode="rope", out_dtype=BF16, rope=rope,
                         n_rope_cols=(nq + nkv) * HEAD_DIM)
            qkv_ctx = matmul(a_ctx, w_qkv, out_dtype=BF16)
            o_lat = attention(qkv, qkv_ctx, attn_sink[j], nq=nq, nkv=nkv, windowed=True)
            h_lat = matmul(o_lat, w_o, mode="res", res=h_lat, gate=g1)
            if ctx_update:
                o_ctx = attention(qkv_ctx, qkv_ctx, attn_sink[j], nq=nq, nkv=nkv, windowed=False)
                h_ctx = matmul(o_ctx, w_o, mode="res", res=h_ctx, gate=cg1)
        else:
            w_in = hy_w_in[j].astype(BF16)
            w_out = hy_w_out[j].astype(BF16)
            f_args = (hy_f_w1[j], hy_f_b1[j], hy_f_w2[j], hy_f_b2[j], hy_f_w3[j], hy_decay[j])
            seqs = [(a_lat, h_lat, g1)]
            if ctx_update:
                seqs.append((modulate(h_ctx, norm_mix_g[i], csh1, csc1), h_ctx, cg1))
            outs = []
            for a_seq, h_seq, gate in seqs:
                Ls = a_seq.shape[1]
                fwd, inv, _ = _dft_matrices(Ls)
                fwd, inv = jnp.asarray(fwd, BF16), jnp.asarray(inv, BF16)
                planes = hyena_spectra(Ls, fwd, hyena_filters(Ls, *f_args))
                outs.append(hyena_mix(a_seq, h_seq, gate, w_in, hy_conv_w[j], hy_conv_b[j],
                                      w_out, hy_fbias[j], (fwd, inv), planes))
            h_lat = outs[0]
            if ctx_update:
                h_ctx = outs[1]

        w_q = peer_w_q[i].astype(BF16)
        uv = jnp.concatenate([peer_u[i], peer_v[i]], axis=1)
        h_lat = peer_mix(h_lat, norm_ffn_g[i], sh2, sc2, g2, w_q, peer_keys1[i], peer_keys2[i], uv)
        if ctx_update:
            h_ctx = peer_mix(h_ctx, norm_ffn_g[i], csh2, csc2, cg2, w_q, peer_keys1[i],
                             peer_keys2[i], uv)

    zero = jnp.zeros((B, 1, D), F32)
    return modulate(h_lat, final_g, zero, zero, out_dtype=F32)
```

```python
import functools
import math

import numpy as np
import jax
import jax.numpy as jnp
from jax import lax
from jax.experimental import pallas as pl
from jax.experimental.pallas import tpu as pltpu

F32 = jnp.float32
BF16 = jnp.bfloat16

LANES = 128
HEAD_DIM = 128
GRID_W = 64
ATTN_BLOCK = 128
ROPE_BASE = 10000.0
RMS_EPS = 1e-6
NEG_INF = -1e30
N_MODS = 6
HY_EMB_BANDS = 16
HY_MOD_SHIFT = 0.05
PEER_TOPK = 16
VMEM_LIMIT = 56 * 2 ** 20
NT_DIMS = (((1,), (1,)), ((), ()))


def _params(*sem):
    return pltpu.CompilerParams(dimension_semantics=sem, vmem_limit_bytes=VMEM_LIMIT)


def _tile(n, pref):
    if n <= pref:
        return n
    t = (pref // LANES) * LANES
    while n % t:
        t -= LANES
    return t


def _modulate_body(h_ref, g_ref, sh_ref, sc_ref, *o_refs):
    x = h_ref[...]
    ms = jnp.mean(x * x, axis=-1, keepdims=True)
    y = x * lax.rsqrt(ms + RMS_EPS) * g_ref[...]
    y = y * (1.0 + sc_ref[...]) + sh_ref[...]
    for o_ref in o_refs:
        o_ref[...] = y.astype(o_ref.dtype)


def modulate(h, g, shift, scale, out_dtypes=None):
    B, L, D = h.shape
    out_dtypes = out_dtypes or (BF16,)
    tm = _tile(L, 512)
    vec = pl.BlockSpec((None, 1, D), lambda b, i: (b, 0, 0))
    blk = pl.BlockSpec((None, tm, D), lambda b, i: (b, i, 0))
    outs = pl.pallas_call(
        _modulate_body,
        grid=(B, L // tm),
        in_specs=[blk, pl.BlockSpec((1, D), lambda b, i: (0, 0)), vec, vec],
        out_specs=[blk] * len(out_dtypes),
        out_shape=[jax.ShapeDtypeStruct((B, L, D), dt) for dt in out_dtypes],
        compiler_params=_params("parallel", "parallel"),
        name="modulate",
    )(h, g.reshape(1, D), shift, scale)
    return outs[0] if len(outs) == 1 else outs


def _mm_plain_body(a_ref, w_ref, o_ref):
    acc = jnp.dot(a_ref[...], w_ref[...].astype(BF16), preferred_element_type=F32)
    o_ref[...] = acc.astype(o_ref.dtype)


def _mm_res_body(a_ref, w_ref, r_ref, g_ref, o_ref):
    acc = jnp.dot(a_ref[...], w_ref[...].astype(BF16), preferred_element_type=F32)
    o_ref[...] = r_ref[...] + g_ref[...] * acc


def _mm_rope_body(a_ref, w_ref, cos_ref, sa_ref, sb_ref, o_ref, *, n_rope):
    acc = jnp.dot(a_ref[...], w_ref[...].astype(BF16), preferred_element_type=F32)
    j = pl.program_id(2)

    @pl.when(j < n_rope)
    def _():
        cos, sa, sb = cos_ref[...], sa_ref[...], sb_ref[...]
        quarter = HEAD_DIM // 4
        for s in range(acc.shape[1] // HEAD_DIM):
            x = acc[:, s * HEAD_DIM:(s + 1) * HEAD_DIM]
            y = (x * cos + pltpu.roll(x, quarter, axis=1) * sa
                 + pltpu.roll(x, HEAD_DIM - quarter, axis=1) * sb)
            o_ref[:, s * HEAD_DIM:(s + 1) * HEAD_DIM] = y.astype(o_ref.dtype)

    @pl.when(j >= n_rope)
    def _():
        o_ref[...] = acc.astype(o_ref.dtype)


def _mm_conv3_body(a_ref, w_ref, cw_ref, cb_ref, o_ref):
    y = jnp.dot(a_ref[...], w_ref[...].astype(BF16), preferred_element_type=F32)
    L = y.shape[0]
    row = lax.broadcasted_iota(jnp.int32, y.shape, 0)
    prev = jnp.where(row == 0, 0.0, pltpu.roll(y, 1, axis=0))
    nxt = jnp.where(row == L - 1, 0.0, pltpu.roll(y, L - 1, axis=0))
    o_ref[...] = cb_ref[...] + prev * cw_ref[0:1, :] + y * cw_ref[1:2, :] + nxt * cw_ref[2:3, :]


def matmul(a, w, *, mode="plain", out_dtype=F32, tm=1024, tn=512, res=None, gate=None,
           rope=None, n_rope_cols=0, conv_w=None, conv_b=None):
    B, L, K = a.shape
    N = w.shape[1]
    tm = L if mode == "conv3" else _tile(L, tm)
    if mode == "rope":
        tn = _tile(math.gcd(N, n_rope_cols), tn)
    else:
        tn = _tile(N, 256 if mode == "conv3" else tn)
    grid = (B, L // tm, N // tn)
    a_spec = pl.BlockSpec((None, tm, K), lambda b, i, j: (b, i, 0))
    w_spec = pl.BlockSpec((K, tn), lambda b, i, j: (0, j))
    o_spec = pl.BlockSpec((None, tm, tn), lambda b, i, j: (b, i, j))
    in_specs, args = [a_spec, w_spec], [a, w]
    if mode == "plain":
        body = _mm_plain_body
    elif mode == "res":
        body, out_dtype = _mm_res_body, F32
        in_specs += [o_spec, pl.BlockSpec((None, 1, tn), lambda b, i, j: (b, 0, j))]
        args += [res, gate]
    elif mode == "rope":
        assert n_rope_cols % tn == 0
        body = functools.partial(_mm_rope_body, n_rope=n_rope_cols // tn)
        tab = pl.BlockSpec((tm, HEAD_DIM), lambda b, i, j: (i, 0))
        in_specs += [tab, tab, tab]
        args += list(rope)
    elif mode == "conv3":
        body, out_dtype = _mm_conv3_body, F32
        in_specs += [pl.BlockSpec((3, tn), lambda b, i, j: (0, j)),
                     pl.BlockSpec((1, tn), lambda b, i, j: (0, j))]
        args += [conv_w, conv_b.reshape(1, N)]
    else:
        raise ValueError(mode)
    return pl.pallas_call(
        body, grid=grid, in_specs=in_specs, out_specs=o_spec,
        out_shape=jax.ShapeDtypeStruct((B, L, N), out_dtype),
        compiler_params=_params("parallel", "parallel", "arbitrary"),
        name="mm_" + mode,
    )(*args)


def _attn_body(sink_ref, q_ref, *refs, nkv, grp, nb, windowed):
    if windowed:
        kp_ref, kc_ref, kn_ref, kx_ref, o_ref = refs
    else:
        kx_ref, o_ref = refs
    n = pl.program_id(1)
    scale = HEAD_DIM ** -0.5
    rows = grp * ATTN_BLOCK
    a = lax.broadcasted_iota(jnp.int32, (rows, ATTN_BLOCK), 0) % ATTN_BLOCK
    kb = lax.broadcasted_iota(jnp.int32, (rows, ATTN_BLOCK), 1)

    def head_cols(ref, c):
        return ref[:, c * HEAD_DIM:(c + 1) * HEAD_DIM]

    for h in range(nkv):
        qh = jnp.concatenate([head_cols(q_ref, h * grp + g) for g in range(grp)], axis=0)
        sink = jnp.concatenate(
            [jnp.full((ATTN_BLOCK, 1), sink_ref[h * grp + g], F32) for g in range(grp)], axis=0)

        def scores(ref):
            return lax.dot_general(qh, head_cols(ref, h), NT_DIMS, preferred_element_type=F32) * scale

        pieces = []
        if windowed:
            s_p = jnp.where(jnp.logical_and(kb >= a, n > 0), scores(kp_ref), NEG_INF)
            s_n = jnp.where(jnp.logical_and(kb <= a, n < nb - 1), scores(kn_ref), NEG_INF)
            pieces += [(s_p, kp_ref), (scores(kc_ref), kc_ref), (s_n, kn_ref)]
        pieces.append((scores(kx_ref), kx_ref))

        m = sink
        for s, _ in pieces:
            m = jnp.maximum(m, jnp.max(s, axis=-1, keepdims=True))
        den = jnp.exp(sink - m)
        o = jnp.zeros((rows, HEAD_DIM), F32)
        for s, ref in pieces:
            p = jnp.exp(s - m)
            den = den + jnp.sum(p, axis=-1, keepdims=True)
            o = o + jnp.dot(p.astype(BF16), head_cols(ref, nkv + h), preferred_element_type=F32)
        o = o / den
        for g in range(grp):
            c = h * grp + g
            o_ref[:, c * HEAD_DIM:(c + 1) * HEAD_DIM] = (
                o[g * ATTN_BLOCK:(g + 1) * ATTN_BLOCK].astype(o_ref.dtype))


def attention(qkv, qkv_ctx, sink, *, nq, nkv, windowed):
    B, L, _ = qkv.shape
    C = qkv_ctx.shape[1]
    nb = L // ATTN_BLOCK
    qd, kvd = nq * HEAD_DIM, nkv * HEAD_DIM
    assert qd % (2 * kvd) == 0
    kv_col = qd // (2 * kvd)
    q_spec = pl.BlockSpec((None, ATTN_BLOCK, qd), lambda b, n, s: (b, n, 0))

    def kv_spec(off):
        return pl.BlockSpec(
            (None, ATTN_BLOCK, 2 * kvd),
            lambda b, n, s: (b, jnp.clip(n + off, 0, nb - 1), kv_col))

    ctx_spec = pl.BlockSpec((None, C, 2 * kvd), lambda b, n, s: (b, 0, kv_col))
    if windowed:
        in_specs = [q_spec, kv_spec(-1), kv_spec(0), kv_spec(1), ctx_spec]
        args = [qkv, qkv, qkv, qkv, qkv_ctx]
    else:
        in_specs, args = [q_spec, ctx_spec], [qkv, qkv_ctx]
    body = functools.partial(_attn_body, nkv=nkv, grp=nq // nkv, nb=nb, windowed=windowed)
    return pl.pallas_call(
        body,
        grid_spec=pltpu.PrefetchScalarGridSpec(
            num_scalar_prefetch=1, grid=(B, nb), in_specs=in_specs,
            out_specs=pl.BlockSpec((None, ATTN_BLOCK, qd), lambda b, n, s: (b, n, 0))),
        out_shape=jax.ShapeDtypeStruct((B, L, qd), BF16),
        compiler_params=_params("parallel", "parallel"),
        name="attn_win" if windowed else "attn_ctx",
    )(sink.astype(F32), *args)


def rope_tables(L):
    rows = L // GRID_W
    row = jnp.repeat(jnp.arange(rows), GRID_W).astype(F32)
    col = jnp.tile(jnp.arange(GRID_W), rows).astype(F32)
    n_freq = HEAD_DIM // 4
    inv = ROPE_BASE ** (-jnp.arange(n_freq, dtype=F32) / n_freq)
    ar, ac = row[:, None] * inv, col[:, None] * inv
    ang = jnp.concatenate([ar, ar, ac, ac], axis=-1)
    cos, sin = jnp.cos(ang), jnp.sin(ang)
    odd = (jnp.arange(HEAD_DIM) // n_freq) % 2 == 1
    return cos, jnp.where(odd, sin, 0.0), jnp.where(odd, 0.0, -sin)


@functools.lru_cache(maxsize=None)
def _dft_matrices(L):
    N = 2 * L
    fb = min(256, L)
    nfb = L // fb
    k = np.arange(L, dtype=np.int64)[:, None]
    t = np.arange(L, dtype=np.int64)[None, :]
    ang = 2.0 * np.pi * ((k * t) % N).astype(np.float64) / N
    c, s = np.cos(ang), np.sin(ang)
    nyq = np.where(np.arange(L) % 2 == 0, 1.0, -1.0)
    f_re, f_im = c.copy(), -s
    f_im[0] = nyq
    i_re, i_im = 2.0 * c / N, -2.0 * s / N
    i_re[0] = 1.0 / N
    i_im[0] = nyq / N
    fwd = np.concatenate([f_re.reshape(nfb, fb, L), f_im.reshape(nfb, fb, L)], axis=1)
    inv = np.concatenate([i_re.reshape(nfb, fb, L), i_im.reshape(nfb, fb, L)], axis=1)
    inv = np.transpose(inv, (0, 2, 1))
    return fwd.astype(np.float32), inv.astype(np.float32), fb


def _filter_body(z_ref, w1_ref, b1_ref, w2_ref, b2_ref, w3_ref, dec_ref, o_ref, *, n_inner):
    hi = lax.Precision.HIGHEST
    h = jnp.sin(jnp.dot(z_ref[...], w1_ref[...], precision=hi, preferred_element_type=F32)
                + b1_ref[...])
    for i in range(n_inner):
        h = jnp.sin(jnp.dot(h, w2_ref[i], precision=hi, preferred_element_type=F32)
                    + b2_ref[i])
    y = jnp.dot(h, w3_ref[...], precision=hi, preferred_element_type=F32)
    L = y.shape[0]
    t01 = lax.broadcasted_iota(jnp.int32, y.shape, 0).astype(F32) / L
    o_ref[...] = y * (jnp.exp(-t01 * jnp.abs(dec_ref[...])) + HY_MOD_SHIFT)


def hyena_filters(L, f_w1, f_b1, f_w2, f_b2, f_w3, decay):
    order, D = decay.shape
    hid = f_w1.shape[1]
    n_inner = f_w2.shape[0]
    n_dir = f_w3.shape[1] // (order * D)
    t = np.arange(L, dtype=np.float32)
    bands = np.arange(1, HY_EMB_BANDS + 1, dtype=np.float32)
    ang = (np.float32(2.0 * math.pi) * t[:, None] * bands[None, :] / np.float32(L)).astype(np.float32)
    z = np.concatenate([(t / np.float32(L))[:, None], np.cos(ang), np.sin(ang)], axis=-1)
    emb = z.shape[1]
    emb_pad = -(-emb // 8) * 8
    z = np.pad(z, ((0, 0), (0, emb_pad - emb))).astype(np.float32)
    w1 = jnp.pad(f_w1, ((0, emb_pad - emb), (0, 0)))
    tn = _tile(D, 1024)
    per_o = n_dir * D // tn
    full = lambda shape: pl.BlockSpec(shape, lambda j: (0,) * len(shape))
    out = pl.pallas_call(
        functools.partial(_filter_body, n_inner=n_inner),
        grid=(order * n_dir * D // tn,),
        in_specs=[full((L, emb_pad)), full((emb_pad, hid)), full((1, hid)),
                  full((n_inner, hid, hid)), full((n_inner, 1, hid)),
                  pl.BlockSpec((hid, tn), lambda j: (0, j)),
                  pl.BlockSpec((None, 1, tn), lambda j: (j // per_o, 0, j % (D // tn)))],
        out_specs=pl.BlockSpec((L, tn), lambda j: (0, j)),
        out_shape=jax.ShapeDtypeStruct((L, order * n_dir * D), F32),
        compiler_params=_params("parallel"),
        name="hyena_filter",
    )(jnp.asarray(z), w1, f_b1.reshape(1, hid), f_w2, f_b2.reshape(n_inner, 1, hid), f_w3,
      decay.reshape(order, 1, D))
    return out.reshape(L, order, n_dir, D)


def hyena_spectra(L, fwd, filt):
    order, D = filt.shape[1], filt.shape[3]
    nfb, fb2, _ = fwd.shape
    fb = fb2 // 2
    f = filt[:, :, 0]
    g = filt[:, :, 1].at[0].set(0.0)
    cols = jnp.stack([f + g, f - g], axis=2).reshape(L, order * 2 * D)
    hi = cols.astype(BF16)
    lo = (cols - hi.astype(F32)).astype(BF16)
    spec = matmul(fwd.reshape(1, nfb * fb2, L), jnp.concatenate([hi, lo], axis=1))
    spec = spec.reshape(nfb, fb2, 2, order, 2, D)
    spec = spec[:, :, 0] + spec[:, :, 1]
    planes = []
    for o in range(order):
        a = spec[:, :fb, o, 0]
        b = spec[:, fb:, o, 1].at[0, 0].set(0.0)
        c = a.at[0, 0].set(spec[0, fb, o, 0])
        planes.append((a, b, c))
    return planes


def _lconv_body(v_ref, x_ref, fw_ref, iv_ref, ha_ref, hb_ref, hc_ref, bias_ref, o_ref,
                acc_ref, vb_ref, *, fb):
    f = pl.program_id(2)

    @pl.when(f == 0)
    def _():
        vb_ref[...] = v_ref[...].astype(BF16)
        acc_ref[...] = jnp.zeros_like(acc_ref)

    spec = jnp.dot(fw_ref[...], vb_ref[...], preferred_element_type=F32)
    xr, xi = spec[:fb], spec[fb:]
    a, b, c = ha_ref[...], hb_ref[...], hc_ref[...]
    yr = (xr * a - xi * b).astype(BF16)
    yi = (xr * b + xi * c).astype(BF16)
    acc_ref[...] += (jnp.dot(iv_ref[:, :fb], yr, preferred_element_type=F32)
                     + jnp.dot(iv_ref[:, fb:], yi, preferred_element_type=F32))

    @pl.when(f == pl.num_programs(2) - 1)
    def _():
        o_ref[...] = (x_ref[...] * (acc_ref[...] + v_ref[...] * bias_ref[...])).astype(o_ref.dtype)


def long_conv_gated(src, src_blk, gate, gate_blk, fwd, inv, planes, bias, out_dtype):
    B, L, _ = src.shape
    D = bias.shape[0]
    nfb, fb2, _ = fwd.shape
    fb = fb2 // 2
    tc = _tile(D, 512)
    ncb = D // tc
    ha, hb, hc = planes
    col = lambda blk: pl.BlockSpec((None, L, tc), lambda c, b, f: (b, 0, blk * ncb + c))
    hspec = pl.BlockSpec((None, fb, tc), lambda c, b, f: (f, 0, c))
    return pl.pallas_call(
        functools.partial(_lconv_body, fb=fb),
        grid=(ncb, B, nfb),
        in_specs=[col(src_blk), col(gate_blk),
                  pl.BlockSpec((None, fb2, L), lambda c, b, f: (f, 0, 0)),
                  pl.BlockSpec((None, L, fb2), lambda c, b, f: (f, 0, 0)),
                  hspec, hspec, hspec,
                  pl.BlockSpec((1, tc), lambda c, b, f: (0, c))],
        out_specs=pl.BlockSpec((None, L, tc), lambda c, b, f: (b, 0, c)),
        out_shape=jax.ShapeDtypeStruct((B, L, D), out_dtype),
        scratch_shapes=[pltpu.VMEM((L, tc), F32), pltpu.VMEM((L, tc), BF16)],
        compiler_params=_params("parallel", "parallel", "arbitrary"),
        name="long_conv",
    )(src, gate, fwd, inv, ha, hb, hc, bias.reshape(1, D))


def hyena_mix(a, res, gate, w_in, conv_w, conv_b, w_out, fbias, dft, planes):
    fwd, inv = dft
    u = matmul(a, w_in, mode="conv3", conv_w=conv_w, conv_b=conv_b)
    z = long_conv_gated(u, 0, u, 1, fwd, inv, planes[0], fbias[0], F32)
    y = long_conv_gated(z, 0, u, 2, fwd, inv, planes[1], fbias[1], BF16)
    return matmul(y, w_out, mode="res", res=res, gate=gate)


def _topk_rows(s, k, key=None):
    if key is None:
        key = lax.broadcasted_iota(jnp.int32, s.shape, 0)
    big = jnp.iinfo(jnp.int32).max
    vals, keys = [], []
    for _ in range(k):
        m = jnp.max(s, axis=0, keepdims=True)
        i = jnp.min(jnp.where(s == m, key, big), axis=0, keepdims=True)
        vals.append(m)
        keys.append(i)
        s = jnp.where(key == i, -jnp.inf, s)
    return vals, keys


def _route_body(q_ref, k1_ref, k2_ref, idx_ref, g_ref, *, half, n_keys):
    T = q_ref.shape[0]
    K = PEER_TOPK
    assert K == 16
    r8 = lax.broadcasted_iota(jnp.int32, (8, LANES), 0)
    r16 = lax.broadcasted_iota(jnp.int32, (K, LANES), 0)
    for c in range(T // LANES):
        q = q_ref[c * LANES:(c + 1) * LANES, :]
        s1 = lax.dot_general(k1_ref[...], q[:, :half], NT_DIMS, preferred_element_type=F32)
        s2 = lax.dot_general(k2_ref[...], q[:, half:], NT_DIMS, preferred_element_type=F32)
        v1, i1 = _topk_rows(s1, K)
        v2, i2 = _topk_rows(s2, K)
        v1c, i1c = jnp.concatenate(v1, axis=0), jnp.concatenate(i1, axis=0)
        v2c, i2c = jnp.concatenate(v2, axis=0), jnp.concatenate(i2, axis=0)
        cand = [v1[0] + v2c]
        cid = [i1[0] * n_keys + i2c]
        pos = [r16]
        for i in range(1, 8):
            n = K // (i + 1)
            val = v1[i] + v2c[:8]
            cand.append(val if n >= 8 else jnp.where(r8 < n, val, -jnp.inf))
            cid.append(i1[i] * n_keys + i2c[:8])
            pos.append(i * K + r8)
        cand.append(v1c[8:] + v2[0])
        cid.append(i1c[8:] * n_keys + i2[0])
        pos.append((8 + r8) * K)
        cand, cid, pos = (jnp.concatenate(x, axis=0) for x in (cand, cid, pos))
        top, sel = _topk_rows(cand, K, key=pos)
        ids = [jnp.max(jnp.where(pos == p, cid, -1), axis=0, keepdims=True) for p in sel]
        e = [jnp.exp(t - top[0]) for t in top]
        den = e[0]
        for x in e[1:]:
            den = den + x
        idx_ref[:, c * LANES:(c + 1) * LANES] = jnp.concatenate(ids, axis=0)
        g_ref[:, c * LANES:(c + 1) * LANES] = jnp.concatenate(e, axis=0) / den


def peer_route(q, keys1, keys2):
    M = q.shape[0]
    H, n_keys, half = keys1.shape
    T = _tile(M, 256)
    kspec = pl.BlockSpec((None, n_keys, half), lambda i, h: (h, 0, 0))
    ospec = pl.BlockSpec((None, PEER_TOPK, T), lambda i, h: (h, 0, i))
    return pl.pallas_call(
        functools.partial(_route_body, half=half, n_keys=n_keys),
        grid=(M // T, H),
        in_specs=[pl.BlockSpec((T, 2 * half), lambda i, h: (i, h)), kspec, kspec],
        out_specs=[ospec, ospec],
        out_shape=[jax.ShapeDtypeStruct((H, PEER_TOPK, M), jnp.int32),
                   jax.ShapeDtypeStruct((H, PEER_TOPK, M), F32)],
        compiler_params=_params("parallel", "arbitrary"),
        name="peer_route",
    )(q, keys1.astype(BF16), keys2.astype(BF16))


GATHER_GROUP = 4
GATHER_RING = 4
GATHER_SLOTS = GATHER_GROUP * GATHER_RING


def pack_expert_tables(u, v):
    E, D = u.shape
    ub = lax.bitcast_convert_type(u.astype(jnp.bfloat16), jnp.uint16).astype(jnp.uint32)
    vb = lax.bitcast_convert_type(v.astype(jnp.bfloat16), jnp.uint16).astype(jnp.uint32)
    return (ub | (vb << 16)).reshape(E, D // LANES, LANES)


def _gather_body(idx_hbm, wg_ref, x_ref, res_ref, g2_ref, uv_hbm, o_ref, idx_smem, *scratch,
                 n_sel, d):
    slots, (isem, sems) = scratch[:GATHER_SLOTS], scratch[GATHER_SLOTS:]
    sub = d // LANES
    tb = x_ref.shape[0]
    step = pl.program_id(0)
    icp = pltpu.make_async_copy(idx_hbm.at[step], idx_smem, isem)
    icp.start()
    icp.wait()

    def row_copy(e, j, s):
        return pltpu.make_async_copy(uv_hbm.at[e], slots[s].at[:, j, :], sems.at[s])

    def issue_group(g, ring):
        for k in range(GATHER_GROUP):
            t = jnp.minimum(g * GATHER_GROUP + k, tb - 1)
            for j in range(n_sel):
                row_copy(idx_smem[t * n_sel + j], j, ring * GATHER_GROUP + k).start()

    def wait_group(ring):
        for k in range(GATHER_GROUP):
            for j in range(n_sel):
                row_copy(0, j, ring * GATHER_GROUP + k).wait()

    ones = jnp.ones((8, LANES), BF16)

    def compute(t, s):
        x = x_ref[pl.ds(t, 1), :]
        p, vs = None, []
        for q in range(sub):
            word = slots[s][q]
            u_q = lax.bitcast_convert_type(word << 16, F32)
            vs.append(lax.bitcast_convert_type(word & jnp.uint32(0xFFFF0000), F32))
            term = u_q * x[:, q * LANES:(q + 1) * LANES]
            p = term if p is None else p + term
        hi = p.astype(BF16)
        lo = (p - hi.astype(F32)).astype(BF16)
        act = (lax.dot_general(ones, hi, NT_DIMS, preferred_element_type=F32)
               + lax.dot_general(ones, lo, NT_DIMS, preferred_element_type=F32))
        gelu = 0.5 * act * (1.0 + lax.erf(act * (2.0 ** -0.5)))
        w = (wg_ref[pl.ds(t, 1), :] * gelu).astype(BF16)
        v = jnp.concatenate(vs, axis=1).astype(BF16)
        out = jnp.dot(w, v, preferred_element_type=F32)
        o_ref[pl.ds(t, 1), :] = res_ref[pl.ds(t, 1), :] + g2_ref[...] * out[0:1]

    for r in range(GATHER_RING - 1):
        issue_group(r, r)

    def sweep(it, carry):
        for r in range(GATHER_RING):
            g = it * GATHER_RING + r
            wait_group(r)
            issue_group(g + GATHER_RING - 1, (r - 1) % GATHER_RING)
            for k in range(GATHER_GROUP):
                compute(g * GATHER_GROUP + k, r * GATHER_GROUP + k)
        return carry

    lax.fori_loop(0, tb // GATHER_SLOTS, sweep, 0)
    for r in range(GATHER_RING - 1):
        wait_group(r)


def peer_experts(idx, gates, x, res, gate2, uv):
    B, L, D = x.shape
    M = B * L
    n_sel = idx.shape[1]
    tb = _tile(L, 512)
    lb = L // tb
    assert tb % GATHER_SLOTS == 0
    body = functools.partial(_gather_body, n_sel=n_sel, d=D)
    tok = lambda shape_last: pl.BlockSpec((tb, shape_last), lambda i: (i, 0))
    out = pl.pallas_call(
        body,
        grid=(M // tb,),
        in_specs=[pl.BlockSpec(memory_space=pl.ANY), tok(n_sel), tok(D), tok(D),
                  pl.BlockSpec((None, 1, D), lambda i: (i // lb, 0, 0)),
                  pl.BlockSpec(memory_space=pl.ANY)],
        out_specs=tok(D),
        out_shape=jax.ShapeDtypeStruct((M, D), F32),
        scratch_shapes=([pltpu.SMEM((tb * n_sel,), jnp.int32)]
                        + [pltpu.VMEM((D // LANES, n_sel, LANES), jnp.uint32)] * GATHER_SLOTS
                        + [pltpu.SemaphoreType.DMA(()), pltpu.SemaphoreType.DMA((GATHER_SLOTS,))]),
        compiler_params=_params("arbitrary"),
        name="peer_experts",
    )(idx.reshape(M // tb, tb * n_sel), gates, x.reshape(M, D), res.reshape(M, D), gate2, uv)
    return out.reshape(B, L, D)


def peer_mix(h, g, shift, scale, gate2, w_q, keys1, keys2, uv):
    B, L, D = h.shape
    a, a32 = modulate(h, g, shift, scale, out_dtypes=(BF16, F32))
    q = matmul(a, w_q, out_dtype=BF16)
    idx, gates = peer_route(q.reshape(B * L, -1), keys1, keys2)
    n_sel = idx.shape[0] * idx.shape[1]
    idx = idx.reshape(n_sel, B * L).T
    gates = gates.reshape(n_sel, B * L).T
    return peer_experts(idx, gates, a32, h, gate2, uv)


def kernel(x, c, ctx, c_ctx, ada_w, ada_b, norm_mix_g, norm_ffn_g, final_g, attn_w_qkv, attn_w_o, attn_sink, hy_w_in, hy_conv_w, hy_conv_b, hy_f_w1, hy_f_b1, hy_f_w2, hy_f_b2, hy_f_w3, hy_decay, hy_fbias, hy_w_out, peer_w_q, peer_keys1, peer_keys2, peer_u, peer_v):
    B, L, D = x.shape
    C = ctx.shape[1]
    depth = ada_w.shape[0]
    n_mixers = 2
    last_attn = max(i for i in range(depth) if i % n_mixers == 0)
    nq = attn_w_o.shape[1] // HEAD_DIM
    nkv = (attn_w_qkv.shape[2] - nq * HEAD_DIM) // (2 * HEAD_DIM)

    cond = jnp.concatenate([jax.nn.silu(c), jax.nn.silu(c_ctx)[None]], axis=0)
    cond = jnp.pad(cond, ((0, -(B + 1) % 16), (0, 0))).astype(BF16)[None]
    rope = rope_tables(L)
    h_lat, h_ctx = x, ctx

    for i in range(depth):
        is_attn = i % n_mixers == 0
        j = i // n_mixers
        ctx_update = i < last_attn
        mods = matmul(cond, ada_w[i])[0, :B + 1] + ada_b[i]
        lat = [m[:, None, :] for m in jnp.split(mods[:B], N_MODS, axis=-1)]
        cx = [jnp.broadcast_to(m[None], (B, 1, D)) for m in jnp.split(mods[B:], N_MODS, axis=-1)]
        sh1, sc1, g1, sh2, sc2, g2 = lat
        csh1, csc1, cg1, csh2, csc2, cg2 = cx
        a_lat = modulate(h_lat, norm_mix_g[i], sh1, sc1)

        if is_attn:
            w_qkv = attn_w_qkv[j].astype(BF16)
            w_o = attn_w_o[j].astype(BF16)
            a_ctx = modulate(h_ctx, norm_mix_g[i], csh1, csc1)
            qkv = matmul(a_lat, w_qkv, mode="rope", out_dtype=BF16, rope=rope,
                         n_rope_cols=(nq + nkv) * HEAD_DIM)
            qkv_ctx = matmul(a_ctx, w_qkv, out_dtype=BF16)
            o_lat = attention(qkv, qkv_ctx, attn_sink[j], nq=nq, nkv=nkv, windowed=True)
            h_lat = matmul(o_lat, w_o, mode="res", res=h_lat, gate=g1)
            if ctx_update:
                o_ctx = attention(qkv_ctx, qkv_ctx, attn_sink[j], nq=nq, nkv=nkv, windowed=False)
                h_ctx = matmul(o_ctx, w_o, mode="res", res=h_ctx, gate=cg1)
        else:
            w_in = hy_w_in[j].astype(BF16)
            w_out = hy_w_out[j].astype(BF16)
            f_args = (hy_f_w1[j], hy_f_b1[j], hy_f_w2[j], hy_f_b2[j], hy_f_w3[j], hy_decay[j])
            seqs = [(a_lat, h_lat, g1)]
            if ctx_update:
                seqs.append((modulate(h_ctx, norm_mix_g[i], csh1, csc1), h_ctx, cg1))
            outs = []
            for a_seq, h_seq, gate in seqs:
                Ls = a_seq.shape[1]
                fwd, inv, _ = _dft_matrices(Ls)
                fwd, inv = jnp.asarray(fwd, BF16), jnp.asarray(inv, BF16)
                planes = hyena_spectra(Ls, fwd, hyena_filters(Ls, *f_args))
                outs.append(hyena_mix(a_seq, h_seq, gate, w_in, hy_conv_w[j], hy_conv_b[j],
                                      w_out, hy_fbias[j], (fwd, inv), planes))
            h_lat = outs[0]
            if ctx_update:
                h_ctx = outs[1]

        w_q = peer_w_q[i].astype(BF16)
        uv = pack_expert_tables(peer_u[i], peer_v[i])
        h_lat = peer_mix(h_lat, norm_ffn_g[i], sh2, sc2, g2, w_q, peer_keys1[i], peer_keys2[i], uv)
        if ctx_update:
            h_ctx = peer_mix(h_ctx, norm_ffn_g[i], csh2, csc2, cg2, w_q, peer_keys1[i],
                             peer_keys2[i], uv)

    zero = jnp.zeros((B, 1, D), F32)
    return modulate(h_lat, final_g, zero, zero, out_dtypes=(F32,))
```

```python
import functools
import math

import numpy as np
import jax
import jax.numpy as jnp
from jax import lax
from jax.experimental import pallas as pl
from jax.experimental.pallas import tpu as pltpu

F32 = jnp.float32
BF16 = jnp.bfloat16

LANES = 128
HEAD_DIM = 128
GRID_W = 64
ATTN_BLOCK = 128
ROPE_BASE = 10000.0
RMS_EPS = 1e-6
NEG_INF = -1e30
N_MODS = 6
HY_EMB_BANDS = 16
HY_MOD_SHIFT = 0.05
PEER_TOPK = 16
VMEM_LIMIT = 56 * 2 ** 20
NT_DIMS = (((1,), (1,)), ((), ()))


def _params(*sem):
    return pltpu.CompilerParams(dimension_semantics=sem, vmem_limit_bytes=VMEM_LIMIT)


def _tile(n, pref):
    if n <= pref:
        return n
    t = (pref // LANES) * LANES
    while n % t:
        t -= LANES
    return t


def _modulate_body(h_ref, g_ref, sh_ref, sc_ref, *o_refs):
    x = h_ref[...]
    ms = jnp.mean(x * x, axis=-1, keepdims=True)
    y = x * lax.rsqrt(ms + RMS_EPS) * g_ref[...]
    y = y * (1.0 + sc_ref[...]) + sh_ref[...]
    for o_ref in o_refs:
        o_ref[...] = y.astype(o_ref.dtype)


def modulate(h, g, shift, scale, out_dtypes=None):
    B, L, D = h.shape
    out_dtypes = out_dtypes or (BF16,)
    tm = _tile(L, 512)
    vec = pl.BlockSpec((None, 1, D), lambda b, i: (b, 0, 0))
    blk = pl.BlockSpec((None, tm, D), lambda b, i: (b, i, 0))
    outs = pl.pallas_call(
        _modulate_body,
        grid=(B, L // tm),
        in_specs=[blk, pl.BlockSpec((1, D), lambda b, i: (0, 0)), vec, vec],
        out_specs=[blk] * len(out_dtypes),
        out_shape=[jax.ShapeDtypeStruct((B, L, D), dt) for dt in out_dtypes],
        compiler_params=_params("parallel", "parallel"),
        name="modulate",
    )(h, g.reshape(1, D), shift, scale)
    return outs[0] if len(outs) == 1 else outs


def _mm_plain_body(a_ref, w_ref, o_ref):
    acc = jnp.dot(a_ref[...], w_ref[...].astype(BF16), preferred_element_type=F32)
    o_ref[...] = acc.astype(o_ref.dtype)


def _mm_res_body(a_ref, w_ref, r_ref, g_ref, o_ref):
    acc = jnp.dot(a_ref[...], w_ref[...].astype(BF16), preferred_element_type=F32)
    o_ref[...] = r_ref[...] + g_ref[...] * acc


def _mm_rope_body(a_ref, w_ref, cos_ref, sa_ref, sb_ref, o_ref, *, n_rope):
    acc = jnp.dot(a_ref[...], w_ref[...].astype(BF16), preferred_element_type=F32)
    j = pl.program_id(2)

    @pl.when(j < n_rope)
    def _():
        cos, sa, sb = cos_ref[...], sa_ref[...], sb_ref[...]
        quarter = HEAD_DIM // 4
        for s in range(acc.shape[1] // HEAD_DIM):
            x = acc[:, s * HEAD_DIM:(s + 1) * HEAD_DIM]
            y = (x * cos + pltpu.roll(x, quarter, axis=1) * sa
                 + pltpu.roll(x, HEAD_DIM - quarter, axis=1) * sb)
            o_ref[:, s * HEAD_DIM:(s + 1) * HEAD_DIM] = y.astype(o_ref.dtype)

    @pl.when(j >= n_rope)
    def _():
        o_ref[...] = acc.astype(o_ref.dtype)


def _mm_conv3_body(a_ref, w_ref, cw_ref, cb_ref, o_ref):
    y = jnp.dot(a_ref[...], w_ref[...].astype(BF16), preferred_element_type=F32)
    L = y.shape[0]
    row = lax.broadcasted_iota(jnp.int32, y.shape, 0)
    prev = jnp.where(row == 0, 0.0, pltpu.roll(y, 1, axis=0))
    nxt = jnp.where(row == L - 1, 0.0, pltpu.roll(y, L - 1, axis=0))
    o_ref[...] = cb_ref[...] + prev * cw_ref[0:1, :] + y * cw_ref[1:2, :] + nxt * cw_ref[2:3, :]


def matmul(a, w, *, mode="plain", out_dtype=F32, tm=1024, tn=512, res=None, gate=None,
           rope=None, n_rope_cols=0, conv_w=None, conv_b=None):
    B, L, K = a.shape
    N = w.shape[1]
    tm = L if mode == "conv3" else _tile(L, tm)
    if mode == "rope":
        tn = _tile(math.gcd(N, n_rope_cols), tn)
    else:
        tn = _tile(N, 256 if mode == "conv3" else tn)
    grid = (B, L // tm, N // tn)
    a_spec = pl.BlockSpec((None, tm, K), lambda b, i, j: (b, i, 0))
    w_spec = pl.BlockSpec((K, tn), lambda b, i, j: (0, j))
    o_spec = pl.BlockSpec((None, tm, tn), lambda b, i, j: (b, i, j))
    in_specs, args = [a_spec, w_spec], [a, w]
    if mode == "plain":
        body = _mm_plain_body
    elif mode == "res":
        body, out_dtype = _mm_res_body, F32
        in_specs += [o_spec, pl.BlockSpec((None, 1, tn), lambda b, i, j: (b, 0, j))]
        args += [res, gate]
    elif mode == "rope":
        assert n_rope_cols % tn == 0
        body = functools.partial(_mm_rope_body, n_rope=n_rope_cols // tn)
        tab = pl.BlockSpec((tm, HEAD_DIM), lambda b, i, j: (i, 0))
        in_specs += [tab, tab, tab]
        args += list(rope)
    elif mode == "conv3":
        body, out_dtype = _mm_conv3_body, F32
        in_specs += [pl.BlockSpec((3, tn), lambda b, i, j: (0, j)),
                     pl.BlockSpec((1, tn), lambda b, i, j: (0, j))]
        args += [conv_w, conv_b.reshape(1, N)]
    else:
        raise ValueError(mode)
    return pl.pallas_call(
        body, grid=grid, in_specs=in_specs, out_specs=o_spec,
        out_shape=jax.ShapeDtypeStruct((B, L, N), out_dtype),
        compiler_params=_params("parallel", "parallel", "arbitrary"),
        name="mm_" + mode,
    )(*args)


def _attn_body(sink_ref, q_ref, *refs, nkv, grp, nb, windowed):
    if windowed:
        kp_ref, kc_ref, kn_ref, kx_ref, o_ref = refs
    else:
        kx_ref, o_ref = refs
    n = pl.program_id(1)
    scale = HEAD_DIM ** -0.5
    rows = grp * ATTN_BLOCK
    a = lax.broadcasted_iota(jnp.int32, (rows, ATTN_BLOCK), 0) % ATTN_BLOCK
    kb = lax.broadcasted_iota(jnp.int32, (rows, ATTN_BLOCK), 1)

    def head_cols(ref, c):
        return ref[:, c * HEAD_DIM:(c + 1) * HEAD_DIM]

    for h in range(nkv):
        qh = jnp.concatenate([head_cols(q_ref, h * grp + g) for g in range(grp)], axis=0)
        sink = jnp.concatenate(
            [jnp.full((ATTN_BLOCK, 1), sink_ref[h * grp + g], F32) for g in range(grp)], axis=0)

        def scores(ref):
            return lax.dot_general(qh, head_cols(ref, h), NT_DIMS, preferred_element_type=F32) * scale

        pieces = []
        if windowed:
            s_p = jnp.where(jnp.logical_and(kb >= a, n > 0), scores(kp_ref), NEG_INF)
            s_n = jnp.where(jnp.logical_and(kb <= a, n < nb - 1), scores(kn_ref), NEG_INF)
            pieces += [(s_p, kp_ref), (scores(kc_ref), kc_ref), (s_n, kn_ref)]
        pieces.append((scores(kx_ref), kx_ref))

        m = sink
        for s, _ in pieces:
            m = jnp.maximum(m, jnp.max(s, axis=-1, keepdims=True))
        den = jnp.exp(sink - m)
        o = jnp.zeros((rows, HEAD_DIM), F32)
        for s, ref in pieces:
            p = jnp.exp(s - m)
            den = den + jnp.sum(p, axis=-1, keepdims=True)
            o = o + jnp.dot(p.astype(BF16), head_cols(ref, nkv + h), preferred_element_type=F32)
        o = o / den
        for g in range(grp):
            c = h * grp + g
            o_ref[:, c * HEAD_DIM:(c + 1) * HEAD_DIM] = (
                o[g * ATTN_BLOCK:(g + 1) * ATTN_BLOCK].astype(o_ref.dtype))


def attention(qkv, qkv_ctx, sink, *, nq, nkv, windowed):
    B, L, _ = qkv.shape
    C = qkv_ctx.shape[1]
    nb = L // ATTN_BLOCK
    qd, kvd = nq * HEAD_DIM, nkv * HEAD_DIM
    assert qd % (2 * kvd) == 0
    kv_col = qd // (2 * kvd)
    q_spec = pl.BlockSpec((None, ATTN_BLOCK, qd), lambda b, n, s: (b, n, 0))

    def kv_spec(off):
        return pl.BlockSpec(
            (None, ATTN_BLOCK, 2 * kvd),
            lambda b, n, s: (b, jnp.clip(n + off, 0, nb - 1), kv_col))

    ctx_spec = pl.BlockSpec((None, C, 2 * kvd), lambda b, n, s: (b, 0, kv_col))
    if windowed:
        in_specs = [q_spec, kv_spec(-1), kv_spec(0), kv_spec(1), ctx_spec]
        args = [qkv, qkv, qkv, qkv, qkv_ctx]
    else:
        in_specs, args = [q_spec, ctx_spec], [qkv, qkv_ctx]
    body = functools.partial(_attn_body, nkv=nkv, grp=nq // nkv, nb=nb, windowed=windowed)
    return pl.pallas_call(
        body,
        grid_spec=pltpu.PrefetchScalarGridSpec(
            num_scalar_prefetch=1, grid=(B, nb), in_specs=in_specs,
            out_specs=pl.BlockSpec((None, ATTN_BLOCK, qd), lambda b, n, s: (b, n, 0))),
        out_shape=jax.ShapeDtypeStruct((B, L, qd), BF16),
        compiler_params=_params("parallel", "parallel"),
        name="attn_win" if windowed else "attn_ctx",
    )(sink.astype(F32), *args)


def rope_tables(L):
    rows = L // GRID_W
    row = jnp.repeat(jnp.arange(rows), GRID_W).astype(F32)
    col = jnp.tile(jnp.arange(GRID_W), rows).astype(F32)
    n_freq = HEAD_DIM // 4
    inv = ROPE_BASE ** (-jnp.arange(n_freq, dtype=F32) / n_freq)
    ar, ac = row[:, None] * inv, col[:, None] * inv
    ang = jnp.concatenate([ar, ar, ac, ac], axis=-1)
    cos, sin = jnp.cos(ang), jnp.sin(ang)
    odd = (jnp.arange(HEAD_DIM) // n_freq) % 2 == 1
    return cos, jnp.where(odd, sin, 0.0), jnp.where(odd, 0.0, -sin)


@functools.lru_cache(maxsize=None)
def _dft_matrices(L):
    N = 2 * L
    fb = min(256, L)
    nfb = L // fb
    k = np.arange(L, dtype=np.int64)[:, None]
    t = np.arange(L, dtype=np.int64)[None, :]
    ang = 2.0 * np.pi * ((k * t) % N).astype(np.float64) / N
    c, s = np.cos(ang), np.sin(ang)
    nyq = np.where(np.arange(L) % 2 == 0, 1.0, -1.0)
    f_re, f_im = c.copy(), -s
    f_im[0] = nyq
    i_re, i_im = 2.0 * c / N, -2.0 * s / N
    i_re[0] = 1.0 / N
    i_im[0] = nyq / N
    fwd = np.concatenate([f_re.reshape(nfb, fb, L), f_im.reshape(nfb, fb, L)], axis=1)
    inv = np.concatenate([i_re.reshape(nfb, fb, L), i_im.reshape(nfb, fb, L)], axis=1)
    inv = np.transpose(inv, (0, 2, 1))
    return fwd.astype(np.float32), inv.astype(np.float32), fb


def _filter_body(z_ref, w1_ref, b1_ref, w2_ref, b2_ref, w3_ref, dec_ref, o_ref, *, n_inner):
    hi = lax.Precision.HIGHEST
    h = jnp.sin(jnp.dot(z_ref[...], w1_ref[...], precision=hi, preferred_element_type=F32)
                + b1_ref[...])
    for i in range(n_inner):
        h = jnp.sin(jnp.dot(h, w2_ref[i], precision=hi, preferred_element_type=F32)
                    + b2_ref[i])
    y = jnp.dot(h, w3_ref[...], precision=hi, preferred_element_type=F32)
    L = y.shape[0]
    t01 = lax.broadcasted_iota(jnp.int32, y.shape, 0).astype(F32) / L
    o_ref[...] = y * (jnp.exp(-t01 * jnp.abs(dec_ref[...])) + HY_MOD_SHIFT)


def hyena_filters(L, f_w1, f_b1, f_w2, f_b2, f_w3, decay):
    order, D = decay.shape
    hid = f_w1.shape[1]
    n_inner = f_w2.shape[0]
    n_dir = f_w3.shape[1] // (order * D)
    t = np.arange(L, dtype=np.float32)
    bands = np.arange(1, HY_EMB_BANDS + 1, dtype=np.float32)
    ang = (np.float32(2.0 * math.pi) * t[:, None] * bands[None, :] / np.float32(L)).astype(np.float32)
    z = np.concatenate([(t / np.float32(L))[:, None], np.cos(ang), np.sin(ang)], axis=-1)
    emb = z.shape[1]
    emb_pad = -(-emb // 8) * 8
    z = np.pad(z, ((0, 0), (0, emb_pad - emb))).astype(np.float32)
    w1 = jnp.pad(f_w1, ((0, emb_pad - emb), (0, 0)))
    tn = _tile(D, 1024)
    per_o = n_dir * D // tn
    full = lambda shape: pl.BlockSpec(shape, lambda j: (0,) * len(shape))
    out = pl.pallas_call(
        functools.partial(_filter_body, n_inner=n_inner),
        grid=(order * n_dir * D // tn,),
        in_specs=[full((L, emb_pad)), full((emb_pad, hid)), full((1, hid)),
                  full((n_inner, hid, hid)), full((n_inner, 1, hid)),
                  pl.BlockSpec((hid, tn), lambda j: (0, j)),
                  pl.BlockSpec((None, 1, tn), lambda j: (j // per_o, 0, j % (D // tn)))],
        out_specs=pl.BlockSpec((L, tn), lambda j: (0, j)),
        out_shape=jax.ShapeDtypeStruct((L, order * n_dir * D), F32),
        compiler_params=_params("parallel"),
        name="hyena_filter",
    )(jnp.asarray(z), w1, f_b1.reshape(1, hid), f_w2, f_b2.reshape(n_inner, 1, hid), f_w3,
      decay.reshape(order, 1, D))
    return out.reshape(L, order, n_dir, D)


def hyena_spectra(L, fwd, filt):
    order, D = filt.shape[1], filt.shape[3]
    nfb, fb2, _ = fwd.shape
    fb = fb2 // 2
    f = filt[:, :, 0]
    g = filt[:, :, 1].at[0].set(0.0)
    cols = jnp.stack([f + g, f - g], axis=2).reshape(L, order * 2 * D)
    hi = cols.astype(BF16)
    lo = (cols - hi.astype(F32)).astype(BF16)
    spec = matmul(fwd.reshape(1, nfb * fb2, L), jnp.concatenate([hi, lo], axis=1))
    spec = spec.reshape(nfb, fb2, 2, order, 2, D)
    spec = spec[:, :, 0] + spec[:, :, 1]
    planes = []
    for o in range(order):
        a = spec[:, :fb, o, 0]
        b = spec[:, fb:, o, 1].at[0, 0].set(0.0)
        c = a.at[0, 0].set(spec[0, fb, o, 0])
        planes.append((a, b, c))
    return planes


def _lconv_body(v_ref, x_ref, fw_ref, iv_ref, ha_ref, hb_ref, hc_ref, bias_ref, o_ref,
                acc_ref, vb_ref, *, fb):
    f = pl.program_id(2)

    @pl.when(f == 0)
    def _():
        vb_ref[...] = v_ref[...].astype(BF16)
        acc_ref[...] = jnp.zeros_like(acc_ref)

    spec = jnp.dot(fw_ref[...], vb_ref[...], preferred_element_type=F32)
    xr, xi = spec[:fb], spec[fb:]
    a, b, c = ha_ref[...], hb_ref[...], hc_ref[...]
    yr = (xr * a - xi * b).astype(BF16)
    yi = (xr * b + xi * c).astype(BF16)
    acc_ref[...] += (jnp.dot(iv_ref[:, :fb], yr, preferred_element_type=F32)
                     + jnp.dot(iv_ref[:, fb:], yi, preferred_element_type=F32))

    @pl.when(f == pl.num_programs(2) - 1)
    def _():
        o_ref[...] = (x_ref[...] * (acc_ref[...] + v_ref[...] * bias_ref[...])).astype(o_ref.dtype)


def long_conv_gated(src, src_blk, gate, gate_blk, fwd, inv, planes, bias, out_dtype):
    B, L, _ = src.shape
    D = bias.shape[0]
    nfb, fb2, _ = fwd.shape
    fb = fb2 // 2
    tc = _tile(D, 512)
    ncb = D // tc
    ha, hb, hc = planes
    col = lambda blk: pl.BlockSpec((None, L, tc), lambda c, b, f: (b, 0, blk * ncb + c))
    hspec = pl.BlockSpec((None, fb, tc), lambda c, b, f: (f, 0, c))
    return pl.pallas_call(
        functools.partial(_lconv_body, fb=fb),
        grid=(ncb, B, nfb),
        in_specs=[col(src_blk), col(gate_blk),
                  pl.BlockSpec((None, fb2, L), lambda c, b, f: (f, 0, 0)),
                  pl.BlockSpec((None, L, fb2), lambda c, b, f: (f, 0, 0)),
                  hspec, hspec, hspec,
                  pl.BlockSpec((1, tc), lambda c, b, f: (0, c))],
        out_specs=pl.BlockSpec((None, L, tc), lambda c, b, f: (b, 0, c)),
        out_shape=jax.ShapeDtypeStruct((B, L, D), out_dtype),
        scratch_shapes=[pltpu.VMEM((L, tc), F32), pltpu.VMEM((L, tc), BF16)],
        compiler_params=_params("parallel", "parallel", "arbitrary"),
        name="long_conv",
    )(src, gate, fwd, inv, ha, hb, hc, bias.reshape(1, D))


def hyena_mix(a, res, gate, w_in, conv_w, conv_b, w_out, fbias, dft, planes):
    fwd, inv = dft
    u = matmul(a, w_in, mode="conv3", conv_w=conv_w, conv_b=conv_b)
    z = long_conv_gated(u, 0, u, 1, fwd, inv, planes[0], fbias[0], F32)
    y = long_conv_gated(z, 0, u, 2, fwd, inv, planes[1], fbias[1], BF16)
    return matmul(y, w_out, mode="res", res=res, gate=gate)


def _topk_rows(s, k, key=None):
    if key is None:
        key = lax.broadcasted_iota(jnp.int32, s.shape, 0)
    big = jnp.iinfo(jnp.int32).max
    vals, keys = [], []
    for _ in range(k):
        m = jnp.max(s, axis=0, keepdims=True)
        i = jnp.min(jnp.where(s == m, key, big), axis=0, keepdims=True)
        vals.append(m)
        keys.append(i)
        s = jnp.where(key == i, -jnp.inf, s)
    return vals, keys


def _route_body(q_ref, k1_ref, k2_ref, idx_ref, g_ref, *, half, n_keys):
    T = q_ref.shape[0]
    K = PEER_TOPK
    assert K == 16
    r8 = lax.broadcasted_iota(jnp.int32, (8, LANES), 0)
    r16 = lax.broadcasted_iota(jnp.int32, (K, LANES), 0)
    for c in range(T // LANES):
        q = q_ref[c * LANES:(c + 1) * LANES, :]
        s1 = lax.dot_general(k1_ref[...], q[:, :half], NT_DIMS, preferred_element_type=F32)
        s2 = lax.dot_general(k2_ref[...], q[:, half:], NT_DIMS, preferred_element_type=F32)
        v1, i1 = _topk_rows(s1, K)
        v2, i2 = _topk_rows(s2, K)
        v1c, i1c = jnp.concatenate(v1, axis=0), jnp.concatenate(i1, axis=0)
        v2c, i2c = jnp.concatenate(v2, axis=0), jnp.concatenate(i2, axis=0)
        cand = [v1[0] + v2c]
        cid = [i1[0] * n_keys + i2c]
        pos = [r16]
        for i in range(1, 8):
            n = K // (i + 1)
            val = v1[i] + v2c[:8]
            cand.append(val if n >= 8 else jnp.where(r8 < n, val, -jnp.inf))
            cid.append(i1[i] * n_keys + i2c[:8])
            pos.append(i * K + r8)
        cand.append(v1c[8:] + v2[0])
        cid.append(i1c[8:] * n_keys + i2[0])
        pos.append((8 + r8) * K)
        cand, cid, pos = (jnp.concatenate(x, axis=0) for x in (cand, cid, pos))
        top, sel = _topk_rows(cand, K, key=pos)
        ids = [jnp.max(jnp.where(pos == p, cid, -1), axis=0, keepdims=True) for p in sel]
        e = [jnp.exp(t - top[0]) for t in top]
        den = e[0]
        for x in e[1:]:
            den = den + x
        idx_ref[:, c * LANES:(c + 1) * LANES] = jnp.concatenate(ids, axis=0)
        g_ref[:, c * LANES:(c + 1) * LANES] = jnp.concatenate(e, axis=0) / den


def peer_route(q, keys1, keys2):
    M = q.shape[0]
    H, n_keys, half = keys1.shape
    T = _tile(M, 256)
    kspec = pl.BlockSpec((None, n_keys, half), lambda i, h: (h, 0, 0))
    ospec = pl.BlockSpec((None, PEER_TOPK, T), lambda i, h: (h, 0, i))
    return pl.pallas_call(
        functools.partial(_route_body, half=half, n_keys=n_keys),
        grid=(M // T, H),
        in_specs=[pl.BlockSpec((T, 2 * half), lambda i, h: (i, h)), kspec, kspec],
        out_specs=[ospec, ospec],
        out_shape=[jax.ShapeDtypeStruct((H, PEER_TOPK, M), jnp.int32),
                   jax.ShapeDtypeStruct((H, PEER_TOPK, M), F32)],
        compiler_params=_params("parallel", "arbitrary"),
        name="peer_route",
    )(q, keys1.astype(BF16), keys2.astype(BF16))


GATHER_GROUP = 4
GATHER_RING = 4
GATHER_SLOTS = GATHER_GROUP * GATHER_RING


def pack_expert_tables(u, v):
    E, D = u.shape
    ub = lax.bitcast_convert_type(u.astype(jnp.bfloat16), jnp.uint16).astype(jnp.uint32)
    vb = lax.bitcast_convert_type(v.astype(jnp.bfloat16), jnp.uint16).astype(jnp.uint32)
    return (ub | (vb << 16)).reshape(E, D // LANES, LANES)


def _gather_body(idx_hbm, wg_ref, x_ref, res_ref, g2_ref, uv_hbm, o_ref, idx_smem, *scratch,
                 n_sel, d):
    slots, (isem, sems) = scratch[:GATHER_SLOTS], scratch[GATHER_SLOTS:]
    sub = d // LANES
    tb = x_ref.shape[0]
    n_idx = tb * n_sel
    step = pl.program_id(0)
    last_step = pl.num_programs(0) - 1
    cur = (step % 2) * n_idx
    nxt = n_idx - cur

    def idx_copy(blk, off):
        return pltpu.make_async_copy(idx_hbm.at[blk], idx_smem.at[pl.ds(off, n_idx)], isem)

    def row_copy(e, j, s):
        return pltpu.make_async_copy(uv_hbm.at[e], slots[s].at[:, j, :], sems.at[s])

    def issue_group(g, ring):
        for k in range(GATHER_GROUP):
            t = g * GATHER_GROUP + k
            beyond = jnp.where(step == last_step, cur + (tb - 1) * n_sel, nxt + (t - tb) * n_sel)
            base = jnp.where(t < tb, cur + t * n_sel, beyond)
            for j in range(n_sel):
                row_copy(idx_smem[base + j], j, ring * GATHER_GROUP + k).start()

    def wait_group(ring):
        for k in range(GATHER_GROUP):
            for j in range(n_sel):
                row_copy(0, j, ring * GATHER_GROUP + k).wait()

    ones = jnp.ones((8, LANES), BF16)

    def compute(t, s):
        x = x_ref[pl.ds(t, 1), :]
        p, vs = None, []
        for q in range(sub):
            word = slots[s][q]
            u_q = lax.bitcast_convert_type(word << 16, F32)
            vs.append(lax.bitcast_convert_type(word & jnp.uint32(0xFFFF0000), F32))
            term = u_q * x[:, q * LANES:(q + 1) * LANES]
            p = term if p is None else p + term
        hi = p.astype(BF16)
        lo = (p - hi.astype(F32)).astype(BF16)
        act = (lax.dot_general(ones, hi, NT_DIMS, preferred_element_type=F32)
               + lax.dot_general(ones, lo, NT_DIMS, preferred_element_type=F32))
        gelu = 0.5 * act * (1.0 + lax.erf(act * (2.0 ** -0.5)))
        w = (wg_ref[pl.ds(t, 1), :] * gelu).astype(BF16)
        v = jnp.concatenate(vs, axis=1).astype(BF16)
        out = jnp.dot(w, v, preferred_element_type=F32)
        o_ref[pl.ds(t, 1), :] = res_ref[pl.ds(t, 1), :] + g2_ref[...] * out[0:1]

    @pl.when(step == 0)
    def _():
        first = idx_copy(0, 0)
        first.start()
        first.wait()
        for r in range(GATHER_RING - 1):
            issue_group(r, r)

    @pl.when(step < last_step)
    def _():
        idx_copy(step + 1, nxt).start()

    n_sweeps = tb // GATHER_SLOTS

    def sweep(it, carry):
        @pl.when(jnp.logical_and(it == n_sweeps - 1, step < last_step))
        def _():
            idx_copy(step + 1, nxt).wait()

        for r in range(GATHER_RING):
            g = it * GATHER_RING + r
            wait_group(r)
            issue_group(g + GATHER_RING - 1, (r - 1) % GATHER_RING)
            for k in range(GATHER_GROUP):
                compute(g * GATHER_GROUP + k, r * GATHER_GROUP + k)
        return carry

    lax.fori_loop(0, n_sweeps, sweep, 0)

    @pl.when(step == last_step)
    def _():
        for r in range(GATHER_RING - 1):
            wait_group(r)


def peer_experts(idx, gates, x, res, gate2, uv):
    B, L, D = x.shape
    M = B * L
    n_sel = idx.shape[1]
    tb = _tile(L, 512)
    lb = L // tb
    assert tb % GATHER_SLOTS == 0
    body = functools.partial(_gather_body, n_sel=n_sel, d=D)
    tok = lambda shape_last: pl.BlockSpec((tb, shape_last), lambda i: (i, 0))
    out = pl.pallas_call(
        body,
        grid=(M // tb,),
        in_specs=[pl.BlockSpec(memory_space=pl.ANY), tok(n_sel), tok(D), tok(D),
                  pl.BlockSpec((None, 1, D), lambda i: (i // lb, 0, 0)),
                  pl.BlockSpec(memory_space=pl.ANY)],
        out_specs=tok(D),
        out_shape=jax.ShapeDtypeStruct((M, D), F32),
        scratch_shapes=([pltpu.SMEM((2 * tb * n_sel,), jnp.int32)]
                        + [pltpu.VMEM((D // LANES, n_sel, LANES), jnp.uint32)] * GATHER_SLOTS
                        + [pltpu.SemaphoreType.DMA(()), pltpu.SemaphoreType.DMA((GATHER_SLOTS,))]),
        compiler_params=_params("arbitrary"),
        name="peer_experts",
    )(idx.reshape(M // tb, tb * n_sel), gates, x.reshape(M, D), res.reshape(M, D), gate2, uv)
    return out.reshape(B, L, D)


def peer_mix(h, g, shift, scale, gate2, w_q, keys1, keys2, uv):
    B, L, D = h.shape
    a, a32 = modulate(h, g, shift, scale, out_dtypes=(BF16, F32))
    q = matmul(a, w_q, out_dtype=BF16)
    idx, gates = peer_route(q.reshape(B * L, -1), keys1, keys2)
    n_sel = idx.shape[0] * idx.shape[1]
    idx = idx.reshape(n_sel, B * L).T
    gates = gates.reshape(n_sel, B * L).T
    return peer_experts(idx, gates, a32, h, gate2, uv)


def kernel(x, c, ctx, c_ctx, ada_w, ada_b, norm_mix_g, norm_ffn_g, final_g, attn_w_qkv, attn_w_o, attn_sink, hy_w_in, hy_conv_w, hy_conv_b, hy_f_w1, hy_f_b1, hy_f_w2, hy_f_b2, hy_f_w3, hy_decay, hy_fbias, hy_w_out, peer_w_q, peer_keys1, peer_keys2, peer_u, peer_v):
    B, L, D = x.shape
    C = ctx.shape[1]
    depth = ada_w.shape[0]
    n_mixers = 2
    last_attn = max(i for i in range(depth) if i % n_mixers == 0)
    nq = attn_w_o.shape[1] // HEAD_DIM
    nkv = (attn_w_qkv.shape[2] - nq * HEAD_DIM) // (2 * HEAD_DIM)

    cond = jnp.concatenate([jax.nn.silu(c), jax.nn.silu(c_ctx)[None]], axis=0)
    cond = jnp.pad(cond, ((0, -(B + 1) % 16), (0, 0))).astype(BF16)[None]
    rope = rope_tables(L)
    h_lat, h_ctx = x, ctx

    for i in range(depth):
        is_attn = i % n_mixers == 0
        j = i // n_mixers
        ctx_update = i < last_attn
        mods = matmul(cond, ada_w[i])[0, :B + 1] + ada_b[i]
        lat = [m[:, None, :] for m in jnp.split(mods[:B], N_MODS, axis=-1)]
        cx = [jnp.broadcast_to(m[None], (B, 1, D)) for m in jnp.split(mods[B:], N_MODS, axis=-1)]
        sh1, sc1, g1, sh2, sc2, g2 = lat
        csh1, csc1, cg1, csh2, csc2, cg2 = cx
        a_lat = modulate(h_lat, norm_mix_g[i], sh1, sc1)

        if is_attn:
            w_qkv = attn_w_qkv[j].astype(BF16)
            w_o = attn_w_o[j].astype(BF16)
            a_ctx = modulate(h_ctx, norm_mix_g[i], csh1, csc1)
            qkv = matmul(a_lat, w_qkv, mode="rope", out_dtype=BF16, rope=rope,
                         n_rope_cols=(nq + nkv) * HEAD_DIM)
            qkv_ctx = matmul(a_ctx, w_qkv, out_dtype=BF16)
            o_lat = attention(qkv, qkv_ctx, attn_sink[j], nq=nq, nkv=nkv, windowed=True)
            h_lat = matmul(o_lat, w_o, mode="res", res=h_lat, gate=g1)
            if ctx_update:
                o_ctx = attention(qkv_ctx, qkv_ctx, attn_sink[j], nq=nq, nkv=nkv, windowed=False)
                h_ctx = matmul(o_ctx, w_o, mode="res", res=h_ctx, gate=cg1)
        else:
            w_in = hy_w_in[j].astype(BF16)
            w_out = hy_w_out[j].astype(BF16)
            f_args = (hy_f_w1[j], hy_f_b1[j], hy_f_w2[j], hy_f_b2[j], hy_f_w3[j], hy_decay[j])
            seqs = [(a_lat, h_lat, g1)]
            if ctx_update:
                seqs.append((modulate(h_ctx, norm_mix_g[i], csh1, csc1), h_ctx, cg1))
            outs = []
            for a_seq, h_seq, gate in seqs:
                Ls = a_seq.shape[1]
                fwd, inv, _ = _dft_matrices(Ls)
                fwd, inv = jnp.asarray(fwd, BF16), jnp.asarray(inv, BF16)
                planes = hyena_spectra(Ls, fwd, hyena_filters(Ls, *f_args))
                outs.append(hyena_mix(a_seq, h_seq, gate, w_in, hy_conv_w[j], hy_conv_b[j],
                                      w_out, hy_fbias[j], (fwd, inv), planes))
            h_lat = outs[0]
            if ctx_update:
                h_ctx = outs[1]

        w_q = peer_w_q[i].astype(BF16)
        uv = pack_expert_tables(peer_u[i], peer_v[i])
        h_lat = peer_mix(h_lat, norm_ffn_g[i], sh2, sc2, g2, w_q, peer_keys1[i], peer_keys2[i], uv)
        if ctx_update:
            h_ctx = peer_mix(h_ctx, norm_ffn_g[i], csh2, csc2, cg2, w_q, peer_keys1[i],
                             peer_keys2[i], uv)

    zero = jnp.zeros((B, 1, D), F32)
    return modulate(h_lat, final_g, zero, zero, out_dtypes=(F32,))
```

```python
import functools
import math

import numpy as np
import jax
import jax.numpy as jnp
from jax import lax
from jax.experimental import pallas as pl
from jax.experimental.pallas import tpu as pltpu

F32 = jnp.float32
BF16 = jnp.bfloat16

LANES = 128
HEAD_DIM = 128
GRID_W = 64
ATTN_BLOCK = 128
ROPE_BASE = 10000.0
RMS_EPS = 1e-6
NEG_INF = -1e30
N_MODS = 6
HY_EMB_BANDS = 16
HY_MOD_SHIFT = 0.05
PEER_TOPK = 16
VMEM_LIMIT = 56 * 2 ** 20
NT_DIMS = (((1,), (1,)), ((), ()))


def _params(*sem):
    return pltpu.CompilerParams(dimension_semantics=sem, vmem_limit_bytes=VMEM_LIMIT)


def _tile(n, pref):
    if n <= pref:
        return n
    t = (pref // LANES) * LANES
    while n % t:
        t -= LANES
    return t


def _modulate_body(h_ref, g_ref, sh_ref, sc_ref, *o_refs):
    x = h_ref[...]
    ms = jnp.mean(x * x, axis=-1, keepdims=True)
    y = x * lax.rsqrt(ms + RMS_EPS) * g_ref[...]
    y = y * (1.0 + sc_ref[...]) + sh_ref[...]
    for o_ref in o_refs:
        o_ref[...] = y.astype(o_ref.dtype)


def modulate(h, g, shift, scale, out_dtypes=None):
    B, L, D = h.shape
    out_dtypes = out_dtypes or (BF16,)
    tm = _tile(L, 512)
    vec = pl.BlockSpec((None, 1, D), lambda b, i: (b, 0, 0))
    blk = pl.BlockSpec((None, tm, D), lambda b, i: (b, i, 0))
    outs = pl.pallas_call(
        _modulate_body,
        grid=(B, L // tm),
        in_specs=[blk, pl.BlockSpec((1, D), lambda b, i: (0, 0)), vec, vec],
        out_specs=[blk] * len(out_dtypes),
        out_shape=[jax.ShapeDtypeStruct((B, L, D), dt) for dt in out_dtypes],
        compiler_params=_params("parallel", "parallel"),
        name="modulate",
    )(h, g.reshape(1, D), shift, scale)
    return outs[0] if len(outs) == 1 else outs


def _mm_plain_body(a_ref, w_ref, o_ref):
    acc = jnp.dot(a_ref[...], w_ref[...].astype(BF16), preferred_element_type=F32)
    o_ref[...] = acc.astype(o_ref.dtype)


def _mm_res_body(a_ref, w_ref, r_ref, g_ref, o_ref):
    acc = jnp.dot(a_ref[...], w_ref[...].astype(BF16), preferred_element_type=F32)
    o_ref[...] = r_ref[...] + g_ref[...] * acc


def _mm_rope_body(a_ref, w_ref, cos_ref, sa_ref, sb_ref, o_ref, *, n_rope):
    acc = jnp.dot(a_ref[...], w_ref[...].astype(BF16), preferred_element_type=F32)
    j = pl.program_id(2)

    @pl.when(j < n_rope)
    def _():
        cos, sa, sb = cos_ref[...], sa_ref[...], sb_ref[...]
        quarter = HEAD_DIM // 4
        for s in range(acc.shape[1] // HEAD_DIM):
            x = acc[:, s * HEAD_DIM:(s + 1) * HEAD_DIM]
            y = (x * cos + pltpu.roll(x, quarter, axis=1) * sa
                 + pltpu.roll(x, HEAD_DIM - quarter, axis=1) * sb)
            o_ref[:, s * HEAD_DIM:(s + 1) * HEAD_DIM] = y.astype(o_ref.dtype)

    @pl.when(j >= n_rope)
    def _():
        o_ref[...] = acc.astype(o_ref.dtype)


def _mm_conv3_body(a_ref, w_ref, cw_ref, cb_ref, o_ref):
    y = jnp.dot(a_ref[...], w_ref[...].astype(BF16), preferred_element_type=F32)
    L = y.shape[0]
    row = lax.broadcasted_iota(jnp.int32, y.shape, 0)
    prev = jnp.where(row == 0, 0.0, pltpu.roll(y, 1, axis=0))
    nxt = jnp.where(row == L - 1, 0.0, pltpu.roll(y, L - 1, axis=0))
    o_ref[...] = cb_ref[...] + prev * cw_ref[0:1, :] + y * cw_ref[1:2, :] + nxt * cw_ref[2:3, :]


def matmul(a, w, *, mode="plain", out_dtype=F32, tm=1024, tn=512, res=None, gate=None,
           rope=None, n_rope_cols=0, conv_w=None, conv_b=None):
    B, L, K = a.shape
    N = w.shape[1]
    tm = L if mode == "conv3" else _tile(L, tm)
    if mode == "rope":
        tn = _tile(math.gcd(N, n_rope_cols), tn)
    else:
        tn = _tile(N, 256 if mode == "conv3" else tn)
    grid = (B, L // tm, N // tn)
    a_spec = pl.BlockSpec((None, tm, K), lambda b, i, j: (b, i, 0))
    w_spec = pl.BlockSpec((K, tn), lambda b, i, j: (0, j))
    o_spec = pl.BlockSpec((None, tm, tn), lambda b, i, j: (b, i, j))
    in_specs, args = [a_spec, w_spec], [a, w]
    if mode == "plain":
        body = _mm_plain_body
    elif mode == "res":
        body, out_dtype = _mm_res_body, F32
        in_specs += [o_spec, pl.BlockSpec((None, 1, tn), lambda b, i, j: (b, 0, j))]
        args += [res, gate]
    elif mode == "rope":
        assert n_rope_cols % tn == 0
        body = functools.partial(_mm_rope_body, n_rope=n_rope_cols // tn)
        tab = pl.BlockSpec((tm, HEAD_DIM), lambda b, i, j: (i, 0))
        in_specs += [tab, tab, tab]
        args += list(rope)
    elif mode == "conv3":
        body, out_dtype = _mm_conv3_body, F32
        in_specs += [pl.BlockSpec((3, tn), lambda b, i, j: (0, j)),
                     pl.BlockSpec((1, tn), lambda b, i, j: (0, j))]
        args += [conv_w, conv_b.reshape(1, N)]
    else:
        raise ValueError(mode)
    return pl.pallas_call(
        body, grid=grid, in_specs=in_specs, out_specs=o_spec,
        out_shape=jax.ShapeDtypeStruct((B, L, N), out_dtype),
        compiler_params=_params("parallel", "parallel", "arbitrary"),
        name="mm_" + mode,
    )(*args)


def _attn_body(sink_ref, q_ref, *refs, nkv, grp, nb, windowed):
    if windowed:
        kp_ref, kc_ref, kn_ref, kx_ref, o_ref = refs
    else:
        kx_ref, o_ref = refs
    n = pl.program_id(1)
    scale = HEAD_DIM ** -0.5
    rows = grp * ATTN_BLOCK
    a = lax.broadcasted_iota(jnp.int32, (rows, ATTN_BLOCK), 0) % ATTN_BLOCK
    kb = lax.broadcasted_iota(jnp.int32, (rows, ATTN_BLOCK), 1)

    def head_cols(ref, c):
        return ref[:, c * HEAD_DIM:(c + 1) * HEAD_DIM]

    for h in range(nkv):
        qh = jnp.concatenate([head_cols(q_ref, h * grp + g) for g in range(grp)], axis=0)
        sink = jnp.concatenate(
            [jnp.full((ATTN_BLOCK, 1), sink_ref[h * grp + g], F32) for g in range(grp)], axis=0)

        def scores(ref):
            return lax.dot_general(qh, head_cols(ref, h), NT_DIMS, preferred_element_type=F32) * scale

        pieces = []
        if windowed:
            s_p = jnp.where(jnp.logical_and(kb >= a, n > 0), scores(kp_ref), NEG_INF)
            s_n = jnp.where(jnp.logical_and(kb <= a, n < nb - 1), scores(kn_ref), NEG_INF)
            pieces += [(s_p, kp_ref), (scores(kc_ref), kc_ref), (s_n, kn_ref)]
        pieces.append((scores(kx_ref), kx_ref))

        m = sink
        for s, _ in pieces:
            m = jnp.maximum(m, jnp.max(s, axis=-1, keepdims=True))
        den = jnp.exp(sink - m)
        o = jnp.zeros((rows, HEAD_DIM), F32)
        for s, ref in pieces:
            p = jnp.exp(s - m)
            den = den + jnp.sum(p, axis=-1, keepdims=True)
            o = o + jnp.dot(p.astype(BF16), head_cols(ref, nkv + h), preferred_element_type=F32)
        o = o / den
        for g in range(grp):
            c = h * grp + g
            o_ref[:, c * HEAD_DIM:(c + 1) * HEAD_DIM] = (
                o[g * ATTN_BLOCK:(g + 1) * ATTN_BLOCK].astype(o_ref.dtype))


def attention(qkv, qkv_ctx, sink, *, nq, nkv, windowed):
    B, L, _ = qkv.shape
    C = qkv_ctx.shape[1]
    nb = L // ATTN_BLOCK
    qd, kvd = nq * HEAD_DIM, nkv * HEAD_DIM
    assert qd % (2 * kvd) == 0
    kv_col = qd // (2 * kvd)
    q_spec = pl.BlockSpec((None, ATTN_BLOCK, qd), lambda b, n, s: (b, n, 0))

    def kv_spec(off):
        return pl.BlockSpec(
            (None, ATTN_BLOCK, 2 * kvd),
            lambda b, n, s: (b, jnp.clip(n + off, 0, nb - 1), kv_col))

    ctx_spec = pl.BlockSpec((None, C, 2 * kvd), lambda b, n, s: (b, 0, kv_col))
    if windowed:
        in_specs = [q_spec, kv_spec(-1), kv_spec(0), kv_spec(1), ctx_spec]
        args = [qkv, qkv, qkv, qkv, qkv_ctx]
    else:
        in_specs, args = [q_spec, ctx_spec], [qkv, qkv_ctx]
    body = functools.partial(_attn_body, nkv=nkv, grp=nq // nkv, nb=nb, windowed=windowed)
    return pl.pallas_call(
        body,
        grid_spec=pltpu.PrefetchScalarGridSpec(
            num_scalar_prefetch=1, grid=(B, nb), in_specs=in_specs,
            out_specs=pl.BlockSpec((None, ATTN_BLOCK, qd), lambda b, n, s: (b, n, 0))),
        out_shape=jax.ShapeDtypeStruct((B, L, qd), BF16),
        compiler_params=_params("parallel", "parallel"),
        name="attn_win" if windowed else "attn_ctx",
    )(sink.astype(F32), *args)


def rope_tables(L):
    rows = L // GRID_W
    row = jnp.repeat(jnp.arange(rows), GRID_W).astype(F32)
    col = jnp.tile(jnp.arange(GRID_W), rows).astype(F32)
    n_freq = HEAD_DIM // 4
    inv = ROPE_BASE ** (-jnp.arange(n_freq, dtype=F32) / n_freq)
    ar, ac = row[:, None] * inv, col[:, None] * inv
    ang = jnp.concatenate([ar, ar, ac, ac], axis=-1)
    cos, sin = jnp.cos(ang), jnp.sin(ang)
    odd = (jnp.arange(HEAD_DIM) // n_freq) % 2 == 1
    return cos, jnp.where(odd, sin, 0.0), jnp.where(odd, 0.0, -sin)


@functools.lru_cache(maxsize=None)
def _dft_matrices(L):
    N = 2 * L
    fb = min(256, L)
    nfb = L // fb
    k = np.arange(L, dtype=np.int64)[:, None]
    t = np.arange(L, dtype=np.int64)[None, :]
    ang = 2.0 * np.pi * ((k * t) % N).astype(np.float64) / N
    c, s = np.cos(ang), np.sin(ang)
    nyq = np.where(np.arange(L) % 2 == 0, 1.0, -1.0)
    f_re, f_im = c.copy(), -s
    f_im[0] = nyq
    i_re, i_im = 2.0 * c / N, -2.0 * s / N
    i_re[0] = 1.0 / N
    i_im[0] = nyq / N
    fwd = np.concatenate([f_re.reshape(nfb, fb, L), f_im.reshape(nfb, fb, L)], axis=1)
    inv = np.concatenate([i_re.reshape(nfb, fb, L), i_im.reshape(nfb, fb, L)], axis=1)
    inv = np.transpose(inv, (0, 2, 1))
    return fwd.astype(np.float32), inv.astype(np.float32), fb


def _filter_body(z_ref, w1_ref, b1_ref, w2_ref, b2_ref, w3_ref, dec_ref, o_ref, *, n_inner):
    hi = lax.Precision.HIGHEST
    h = jnp.sin(jnp.dot(z_ref[...], w1_ref[...], precision=hi, preferred_element_type=F32)
                + b1_ref[...])
    for i in range(n_inner):
        h = jnp.sin(jnp.dot(h, w2_ref[i], precision=hi, preferred_element_type=F32)
                    + b2_ref[i])
    y = jnp.dot(h, w3_ref[...], precision=hi, preferred_element_type=F32)
    L = y.shape[0]
    t01 = lax.broadcasted_iota(jnp.int32, y.shape, 0).astype(F32) / L
    o_ref[...] = y * (jnp.exp(-t01 * jnp.abs(dec_ref[...])) + HY_MOD_SHIFT)


def hyena_filters(L, f_w1, f_b1, f_w2, f_b2, f_w3, decay):
    order, D = decay.shape
    hid = f_w1.shape[1]
    n_inner = f_w2.shape[0]
    n_dir = f_w3.shape[1] // (order * D)
    t = np.arange(L, dtype=np.float32)
    bands = np.arange(1, HY_EMB_BANDS + 1, dtype=np.float32)
    ang = (np.float32(2.0 * math.pi) * t[:, None] * bands[None, :] / np.float32(L)).astype(np.float32)
    z = np.concatenate([(t / np.float32(L))[:, None], np.cos(ang), np.sin(ang)], axis=-1)
    emb = z.shape[1]
    emb_pad = -(-emb // 8) * 8
    z = np.pad(z, ((0, 0), (0, emb_pad - emb))).astype(np.float32)
    w1 = jnp.pad(f_w1, ((0, emb_pad - emb), (0, 0)))
    tn = _tile(D, 1024)
    per_o = n_dir * D // tn
    full = lambda shape: pl.BlockSpec(shape, lambda j: (0,) * len(shape))
    out = pl.pallas_call(
        functools.partial(_filter_body, n_inner=n_inner),
        grid=(order * n_dir * D // tn,),
        in_specs=[full((L, emb_pad)), full((emb_pad, hid)), full((1, hid)),
                  full((n_inner, hid, hid)), full((n_inner, 1, hid)),
                  pl.BlockSpec((hid, tn), lambda j: (0, j)),
                  pl.BlockSpec((None, 1, tn), lambda j: (j // per_o, 0, j % (D // tn)))],
        out_specs=pl.BlockSpec((L, tn), lambda j: (0, j)),
        out_shape=jax.ShapeDtypeStruct((L, order * n_dir * D), F32),
        compiler_params=_params("parallel"),
        name="hyena_filter",
    )(jnp.asarray(z), w1, f_b1.reshape(1, hid), f_w2, f_b2.reshape(n_inner, 1, hid), f_w3,
      decay.reshape(order, 1, D))
    return out.reshape(L, order, n_dir, D)


def hyena_spectra(L, fwd, filt):
    order, D = filt.shape[1], filt.shape[3]
    nfb, fb2, _ = fwd.shape
    fb = fb2 // 2
    f = filt[:, :, 0]
    g = filt[:, :, 1].at[0].set(0.0)
    cols = jnp.stack([f + g, f - g], axis=2).reshape(L, order * 2 * D)
    hi = cols.astype(BF16)
    lo = (cols - hi.astype(F32)).astype(BF16)
    spec = matmul(fwd.reshape(1, nfb * fb2, L), jnp.concatenate([hi, lo], axis=1))
    spec = spec.reshape(nfb, fb2, 2, order, 2, D)
    spec = spec[:, :, 0] + spec[:, :, 1]
    planes = []
    for o in range(order):
        a = spec[:, :fb, o, 0]
        b = spec[:, fb:, o, 1].at[0, 0].set(0.0)
        c = a.at[0, 0].set(spec[0, fb, o, 0])
        planes.append((a, b, c))
    return planes


def _lconv_body(v_ref, x_ref, fw_ref, iv_ref, ha_ref, hb_ref, hc_ref, bias_ref, o_ref,
                acc_ref, vb_ref, *, fb):
    f = pl.program_id(2)

    @pl.when(f == 0)
    def _():
        vb_ref[...] = v_ref[...].astype(BF16)
        acc_ref[...] = jnp.zeros_like(acc_ref)

    spec = jnp.dot(fw_ref[...], vb_ref[...], preferred_element_type=F32)
    xr, xi = spec[:fb], spec[fb:]
    a, b, c = ha_ref[...], hb_ref[...], hc_ref[...]
    yr = (xr * a - xi * b).astype(BF16)
    yi = (xr * b + xi * c).astype(BF16)
    acc_ref[...] += (jnp.dot(iv_ref[:, :fb], yr, preferred_element_type=F32)
                     + jnp.dot(iv_ref[:, fb:], yi, preferred_element_type=F32))

    @pl.when(f == pl.num_programs(2) - 1)
    def _():
        o_ref[...] = (x_ref[...] * (acc_ref[...] + v_ref[...] * bias_ref[...])).astype(o_ref.dtype)


def long_conv_gated(src, src_blk, gate, gate_blk, fwd, inv, planes, bias, out_dtype):
    B, L, _ = src.shape
    D = bias.shape[0]
    nfb, fb2, _ = fwd.shape
    fb = fb2 // 2
    tc = _tile(D, 512)
    ncb = D // tc
    ha, hb, hc = planes
    col = lambda blk: pl.BlockSpec((None, L, tc), lambda c, b, f: (b, 0, blk * ncb + c))
    hspec = pl.BlockSpec((None, fb, tc), lambda c, b, f: (f, 0, c))
    return pl.pallas_call(
        functools.partial(_lconv_body, fb=fb),
        grid=(ncb, B, nfb),
        in_specs=[col(src_blk), col(gate_blk),
                  pl.BlockSpec((None, fb2, L), lambda c, b, f: (f, 0, 0)),
                  pl.BlockSpec((None, L, fb2), lambda c, b, f: (f, 0, 0)),
                  hspec, hspec, hspec,
                  pl.BlockSpec((1, tc), lambda c, b, f: (0, c))],
        out_specs=pl.BlockSpec((None, L, tc), lambda c, b, f: (b, 0, c)),
        out_shape=jax.ShapeDtypeStruct((B, L, D), out_dtype),
        scratch_shapes=[pltpu.VMEM((L, tc), F32), pltpu.VMEM((L, tc), BF16)],
        compiler_params=_params("parallel", "parallel", "arbitrary"),
        name="long_conv",
    )(src, gate, fwd, inv, ha, hb, hc, bias.reshape(1, D))


def hyena_mix(a, res, gate, w_in, conv_w, conv_b, w_out, fbias, dft, planes):
    fwd, inv = dft
    u = matmul(a, w_in, mode="conv3", conv_w=conv_w, conv_b=conv_b)
    z = long_conv_gated(u, 0, u, 1, fwd, inv, planes[0], fbias[0], F32)
    y = long_conv_gated(z, 0, u, 2, fwd, inv, planes[1], fbias[1], BF16)
    return matmul(y, w_out, mode="res", res=res, gate=gate)


def _topk_rows(s, k, key=None):
    if key is None:
        key = lax.broadcasted_iota(jnp.int32, s.shape, 0)
    big = jnp.iinfo(jnp.int32).max
    vals, keys = [], []
    for _ in range(k):
        m = jnp.max(s, axis=0, keepdims=True)
        i = jnp.min(jnp.where(s == m, key, big), axis=0, keepdims=True)
        vals.append(m)
        keys.append(i)
        s = jnp.where(key == i, -jnp.inf, s)
    return vals, keys


def _route_body(q_ref, k1_ref, k2_ref, idx_ref, g_ref, *, half, n_keys):
    T = q_ref.shape[0]
    K = PEER_TOPK
    assert K == 16
    r8 = lax.broadcasted_iota(jnp.int32, (8, LANES), 0)
    r16 = lax.broadcasted_iota(jnp.int32, (K, LANES), 0)
    for c in range(T // LANES):
        q = q_ref[c * LANES:(c + 1) * LANES, :]
        s1 = lax.dot_general(k1_ref[...], q[:, :half], NT_DIMS, preferred_element_type=F32)
        s2 = lax.dot_general(k2_ref[...], q[:, half:], NT_DIMS, preferred_element_type=F32)
        v1, i1 = _topk_rows(s1, K)
        v2, i2 = _topk_rows(s2, K)
        v1c, i1c = jnp.concatenate(v1, axis=0), jnp.concatenate(i1, axis=0)
        v2c, i2c = jnp.concatenate(v2, axis=0), jnp.concatenate(i2, axis=0)
        cand = [v1[0] + v2c]
        cid = [i1[0] * n_keys + i2c]
        pos = [r16]
        for i in range(1, 8):
            n = K // (i + 1)
            val = v1[i] + v2c[:8]
            cand.append(val if n >= 8 else jnp.where(r8 < n, val, -jnp.inf))
            cid.append(i1[i] * n_keys + i2c[:8])
            pos.append(i * K + r8)
        cand.append(v1c[8:] + v2[0])
        cid.append(i1c[8:] * n_keys + i2[0])
        pos.append((8 + r8) * K)
        cand, cid, pos = (jnp.concatenate(x, axis=0) for x in (cand, cid, pos))
        top, sel = _topk_rows(cand, K, key=pos)
        ids = [jnp.max(jnp.where(pos == p, cid, -1), axis=0, keepdims=True) for p in sel]
        e = [jnp.exp(t - top[0]) for t in top]
        den = e[0]
        for x in e[1:]:
            den = den + x
        idx_ref[:, c * LANES:(c + 1) * LANES] = jnp.concatenate(ids, axis=0)
        g_ref[:, c * LANES:(c + 1) * LANES] = jnp.concatenate(e, axis=0) / den


def peer_route(q, keys1, keys2):
    M = q.shape[0]
    H, n_keys, half = keys1.shape
    T = _tile(M, 256)
    kspec = pl.BlockSpec((None, n_keys, half), lambda i, h: (h, 0, 0))
    ospec = pl.BlockSpec((None, PEER_TOPK, T), lambda i, h: (h, 0, i))
    return pl.pallas_call(
        functools.partial(_route_body, half=half, n_keys=n_keys),
        grid=(M // T, H),
        in_specs=[pl.BlockSpec((T, 2 * half), lambda i, h: (i, h)), kspec, kspec],
        out_specs=[ospec, ospec],
        out_shape=[jax.ShapeDtypeStruct((H, PEER_TOPK, M), jnp.int32),
                   jax.ShapeDtypeStruct((H, PEER_TOPK, M), F32)],
        compiler_params=_params("parallel", "arbitrary"),
        name="peer_route",
    )(q, keys1.astype(BF16), keys2.astype(BF16))


GATHER_GROUP = 4
GATHER_RING = 4
GATHER_SLOTS = GATHER_GROUP * GATHER_RING


def pack_expert_tables(u, v):
    E, D = u.shape
    ub = lax.bitcast_convert_type(u.astype(jnp.bfloat16), jnp.uint16).astype(jnp.uint32)
    vb = lax.bitcast_convert_type(v.astype(jnp.bfloat16), jnp.uint16).astype(jnp.uint32)
    return (ub | (vb << 16)).reshape(E, D // LANES, LANES)


def _gather_body(idx_hbm, wg_ref, x_ref, res_ref, g2_ref, uv_hbm, o_ref, idx_smem, *scratch,
                 n_sel, d):
    slots, (isem, sems) = scratch[:GATHER_SLOTS], scratch[GATHER_SLOTS:]
    sub = d // LANES
    tb = x_ref.shape[0]
    n_idx = tb * n_sel
    step = pl.program_id(0)
    last_step = pl.num_programs(0) - 1
    cur = (step % 2) * n_idx
    nxt = n_idx - cur

    def idx_copy(blk, off):
        return pltpu.make_async_copy(idx_hbm.at[blk], idx_smem.at[pl.ds(off, n_idx)], isem)

    def row_copy(e, j, s):
        return pltpu.make_async_copy(uv_hbm.at[e], slots[s].at[:, j, :], sems.at[s])

    def issue_group(g, ring):
        for k in range(GATHER_GROUP):
            t = g * GATHER_GROUP + k
            beyond = jnp.where(step == last_step, cur + (tb - 1) * n_sel, nxt + (t - tb) * n_sel)
            base = jnp.where(t < tb, cur + t * n_sel, beyond)
            for j in range(n_sel):
                row_copy(idx_smem[base + j], j, ring * GATHER_GROUP + k).start(priority=j % 2)

    def wait_group(ring):
        for k in range(GATHER_GROUP):
            for j in range(n_sel):
                row_copy(0, j, ring * GATHER_GROUP + k).wait()

    ones = jnp.ones((8, LANES), BF16)

    def compute(t, s):
        x = x_ref[pl.ds(t, 1), :]
        p, vs = None, []
        for q in range(sub):
            word = slots[s][q]
            u_q = lax.bitcast_convert_type(word << 16, F32)
            vs.append(lax.bitcast_convert_type(word & jnp.uint32(0xFFFF0000), F32))
            term = u_q * x[:, q * LANES:(q + 1) * LANES]
            p = term if p is None else p + term
        hi = p.astype(BF16)
        lo = (p - hi.astype(F32)).astype(BF16)
        act = (lax.dot_general(ones, hi, NT_DIMS, preferred_element_type=F32)
               + lax.dot_general(ones, lo, NT_DIMS, preferred_element_type=F32))
        gelu = 0.5 * act * (1.0 + lax.erf(act * (2.0 ** -0.5)))
        w = (wg_ref[pl.ds(t, 1), :] * gelu).astype(BF16)
        v = jnp.concatenate(vs, axis=1).astype(BF16)
        out = jnp.dot(w, v, preferred_element_type=F32)
        o_ref[pl.ds(t, 1), :] = res_ref[pl.ds(t, 1), :] + g2_ref[...] * out[0:1]

    @pl.when(step == 0)
    def _():
        first = idx_copy(0, 0)
        first.start()
        first.wait()
        for r in range(GATHER_RING - 1):
            issue_group(r, r)

    @pl.when(step < last_step)
    def _():
        idx_copy(step + 1, nxt).start()

    n_sweeps = tb // GATHER_SLOTS

    def sweep(it, carry):
        @pl.when(jnp.logical_and(it == n_sweeps - 1, step < last_step))
        def _():
            idx_copy(step + 1, nxt).wait()

        for r in range(GATHER_RING):
            g = it * GATHER_RING + r
            wait_group(r)
            issue_group(g + GATHER_RING - 1, (r - 1) % GATHER_RING)
            for k in range(GATHER_GROUP):
                compute(g * GATHER_GROUP + k, r * GATHER_GROUP + k)
        return carry

    lax.fori_loop(0, n_sweeps, sweep, 0)

    @pl.when(step == last_step)
    def _():
        for r in range(GATHER_RING - 1):
            wait_group(r)


def peer_experts(idx, gates, x, res, gate2, uv):
    B, L, D = x.shape
    M = B * L
    n_sel = idx.shape[1]
    tb = _tile(L, 512)
    lb = L // tb
    assert tb % GATHER_SLOTS == 0
    body = functools.partial(_gather_body, n_sel=n_sel, d=D)
    tok = lambda shape_last: pl.BlockSpec((tb, shape_last), lambda i: (i, 0))
    out = pl.pallas_call(
        body,
        grid=(M // tb,),
        in_specs=[pl.BlockSpec(memory_space=pl.ANY), tok(n_sel), tok(D), tok(D),
                  pl.BlockSpec((None, 1, D), lambda i: (i // lb, 0, 0)),
                  pl.BlockSpec(memory_space=pl.ANY)],
        out_specs=tok(D),
        out_shape=jax.ShapeDtypeStruct((M, D), F32),
        scratch_shapes=([pltpu.SMEM((2 * tb * n_sel,), jnp.int32)]
                        + [pltpu.VMEM((D // LANES, n_sel, LANES), jnp.uint32)] * GATHER_SLOTS
                        + [pltpu.SemaphoreType.DMA(()), pltpu.SemaphoreType.DMA((GATHER_SLOTS,))]),
        compiler_params=_params("arbitrary"),
        name="peer_experts",
    )(idx.reshape(M // tb, tb * n_sel), gates, x.reshape(M, D), res.reshape(M, D), gate2, uv)
    return out.reshape(B, L, D)


def peer_mix(h, g, shift, scale, gate2, w_q, keys1, keys2, uv):
    B, L, D = h.shape
    a, a32 = modulate(h, g, shift, scale, out_dtypes=(BF16, F32))
    q = matmul(a, w_q, out_dtype=BF16)
    idx, gates = peer_route(q.reshape(B * L, -1), keys1, keys2)
    n_sel = idx.shape[0] * idx.shape[1]
    idx = idx.reshape(n_sel, B * L).T
    gates = gates.reshape(n_sel, B * L).T
    return peer_experts(idx, gates, a32, h, gate2, uv)


def kernel(x, c, ctx, c_ctx, ada_w, ada_b, norm_mix_g, norm_ffn_g, final_g, attn_w_qkv, attn_w_o, attn_sink, hy_w_in, hy_conv_w, hy_conv_b, hy_f_w1, hy_f_b1, hy_f_w2, hy_f_b2, hy_f_w3, hy_decay, hy_fbias, hy_w_out, peer_w_q, peer_keys1, peer_keys2, peer_u, peer_v):
    B, L, D = x.shape
    C = ctx.shape[1]
    depth = ada_w.shape[0]
    n_mixers = 2
    last_attn = max(i for i in range(depth) if i % n_mixers == 0)
    nq = attn_w_o.shape[1] // HEAD_DIM
    nkv = (attn_w_qkv.shape[2] - nq * HEAD_DIM) // (2 * HEAD_DIM)

    cond = jnp.concatenate([jax.nn.silu(c), jax.nn.silu(c_ctx)[None]], axis=0)
    cond = jnp.pad(cond, ((0, -(B + 1) % 16), (0, 0))).astype(BF16)[None]
    rope = rope_tables(L)
    h_lat, h_ctx = x, ctx

    for i in range(depth):
        is_attn = i % n_mixers == 0
        j = i // n_mixers
        ctx_update = i < last_attn
        mods = matmul(cond, ada_w[i])[0, :B + 1] + ada_b[i]
        lat = [m[:, None, :] for m in jnp.split(mods[:B], N_MODS, axis=-1)]
        cx = [jnp.broadcast_to(m[None], (B, 1, D)) for m in jnp.split(mods[B:], N_MODS, axis=-1)]
        sh1, sc1, g1, sh2, sc2, g2 = lat
        csh1, csc1, cg1, csh2, csc2, cg2 = cx
        a_lat = modulate(h_lat, norm_mix_g[i], sh1, sc1)

        if is_attn:
            w_qkv = attn_w_qkv[j].astype(BF16)
            w_o = attn_w_o[j].astype(BF16)
            a_ctx = modulate(h_ctx, norm_mix_g[i], csh1, csc1)
            qkv = matmul(a_lat, w_qkv, mode="rope", out_dtype=BF16, rope=rope,
                         n_rope_cols=(nq + nkv) * HEAD_DIM)
            qkv_ctx = matmul(a_ctx, w_qkv, out_dtype=BF16)
            o_lat = attention(qkv, qkv_ctx, attn_sink[j], nq=nq, nkv=nkv, windowed=True)
            h_lat = matmul(o_lat, w_o, mode="res", res=h_lat, gate=g1)
            if ctx_update:
                o_ctx = attention(qkv_ctx, qkv_ctx, attn_sink[j], nq=nq, nkv=nkv, windowed=False)
                h_ctx = matmul(o_ctx, w_o, mode="res", res=h_ctx, gate=cg1)
        else:
            w_in = hy_w_in[j].astype(BF16)
            w_out = hy_w_out[j].astype(BF16)
            f_args = (hy_f_w1[j], hy_f_b1[j], hy_f_w2[j], hy_f_b2[j], hy_f_w3[j], hy_decay[j])
            seqs = [(a_lat, h_lat, g1)]
            if ctx_update:
                seqs.append((modulate(h_ctx, norm_mix_g[i], csh1, csc1), h_ctx, cg1))
            outs = []
            for a_seq, h_seq, gate in seqs:
                Ls = a_seq.shape[1]
                fwd, inv, _ = _dft_matrices(Ls)
                fwd, inv = jnp.asarray(fwd, BF16), jnp.asarray(inv, BF16)
                planes = hyena_spectra(Ls, fwd, hyena_filters(Ls, *f_args))
                outs.append(hyena_mix(a_seq, h_seq, gate, w_in, hy_conv_w[j], hy_conv_b[j],
                                      w_out, hy_fbias[j], (fwd, inv), planes))
            h_lat = outs[0]
            if ctx_update:
                h_ctx = outs[1]

        w_q = peer_w_q[i].astype(BF16)
        uv = pack_expert_tables(peer_u[i], peer_v[i])
        h_lat = peer_mix(h_lat, norm_ffn_g[i], sh2, sc2, g2, w_q, peer_keys1[i], peer_keys2[i], uv)
        if ctx_update:
            h_ctx = peer_mix(h_ctx, norm_ffn_g[i], csh2, csc2, cg2, w_q, peer_keys1[i],
                             peer_keys2[i], uv)

    zero = jnp.zeros((B, 1, D), F32)
    return modulate(h_lat, final_g, zero, zero, out_dtypes=(F32,))
```

```python
import functools
import math

import numpy as np
import jax
import jax.numpy as jnp
from jax import lax
from jax.experimental import pallas as pl
from jax.experimental.pallas import tpu as pltpu

F32 = jnp.float32
BF16 = jnp.bfloat16

LANES = 128
HEAD_DIM = 128
GRID_W = 64
ATTN_BLOCK = 128
ROPE_BASE = 10000.0
RMS_EPS = 1e-6
NEG_INF = -1e30
N_MODS = 6
HY_EMB_BANDS = 16
HY_MOD_SHIFT = 0.05
PEER_TOPK = 16
VMEM_LIMIT = 56 * 2 ** 20
NT_DIMS = (((1,), (1,)), ((), ()))


def _params(*sem):
    return pltpu.CompilerParams(dimension_semantics=sem, vmem_limit_bytes=VMEM_LIMIT)


def _tile(n, pref):
    if n <= pref:
        return n
    t = (pref // LANES) * LANES
    while n % t:
        t -= LANES
    return t


def _modulate_body(h_ref, g_ref, sh_ref, sc_ref, *o_refs):
    x = h_ref[...]
    ms = jnp.mean(x * x, axis=-1, keepdims=True)
    y = x * lax.rsqrt(ms + RMS_EPS) * g_ref[...]
    y = y * (1.0 + sc_ref[...]) + sh_ref[...]
    for o_ref in o_refs:
        o_ref[...] = y.astype(o_ref.dtype)


def modulate(h, g, shift, scale, out_dtypes=None):
    B, L, D = h.shape
    out_dtypes = out_dtypes or (BF16,)
    tm = _tile(L, 512)
    vec = pl.BlockSpec((None, 1, D), lambda b, i: (b, 0, 0))
    blk = pl.BlockSpec((None, tm, D), lambda b, i: (b, i, 0))
    outs = pl.pallas_call(
        _modulate_body,
        grid=(B, L // tm),
        in_specs=[blk, pl.BlockSpec((1, D), lambda b, i: (0, 0)), vec, vec],
        out_specs=[blk] * len(out_dtypes),
        out_shape=[jax.ShapeDtypeStruct((B, L, D), dt) for dt in out_dtypes],
        compiler_params=_params("parallel", "parallel"),
        name="modulate",
    )(h, g.reshape(1, D), shift, scale)
    return outs[0] if len(outs) == 1 else outs


def _mm_plain_body(a_ref, w_ref, o_ref):
    acc = jnp.dot(a_ref[...], w_ref[...].astype(BF16), preferred_element_type=F32)
    o_ref[...] = acc.astype(o_ref.dtype)


def _mm_res_body(a_ref, w_ref, r_ref, g_ref, o_ref):
    acc = jnp.dot(a_ref[...], w_ref[...].astype(BF16), preferred_element_type=F32)
    o_ref[...] = r_ref[...] + g_ref[...] * acc


def _mm_rope_body(a_ref, w_ref, cos_ref, sa_ref, sb_ref, o_ref, *, n_rope):
    acc = jnp.dot(a_ref[...], w_ref[...].astype(BF16), preferred_element_type=F32)
    j = pl.program_id(2)

    @pl.when(j < n_rope)
    def _():
        cos, sa, sb = cos_ref[...], sa_ref[...], sb_ref[...]
        quarter = HEAD_DIM // 4
        for s in range(acc.shape[1] // HEAD_DIM):
            x = acc[:, s * HEAD_DIM:(s + 1) * HEAD_DIM]
            y = (x * cos + pltpu.roll(x, quarter, axis=1) * sa
                 + pltpu.roll(x, HEAD_DIM - quarter, axis=1) * sb)
            o_ref[:, s * HEAD_DIM:(s + 1) * HEAD_DIM] = y.astype(o_ref.dtype)

    @pl.when(j >= n_rope)
    def _():
        o_ref[...] = acc.astype(o_ref.dtype)


def _mm_conv3_body(a_ref, w_ref, cw_ref, cb_ref, o_ref):
    y = jnp.dot(a_ref[...], w_ref[...].astype(BF16), preferred_element_type=F32)
    L = y.shape[0]
    row = lax.broadcasted_iota(jnp.int32, y.shape, 0)
    prev = jnp.where(row == 0, 0.0, pltpu.roll(y, 1, axis=0))
    nxt = jnp.where(row == L - 1, 0.0, pltpu.roll(y, L - 1, axis=0))
    o_ref[...] = cb_ref[...] + prev * cw_ref[0:1, :] + y * cw_ref[1:2, :] + nxt * cw_ref[2:3, :]


def matmul(a, w, *, mode="plain", out_dtype=F32, tm=1024, tn=512, res=None, gate=None,
           rope=None, n_rope_cols=0, conv_w=None, conv_b=None):
    B, L, K = a.shape
    N = w.shape[1]
    tm = L if mode == "conv3" else _tile(L, tm)
    if mode == "rope":
        tn = _tile(math.gcd(N, n_rope_cols), tn)
    else:
        tn = _tile(N, 256 if mode == "conv3" else tn)
    grid = (B, L // tm, N // tn)
    a_spec = pl.BlockSpec((None, tm, K), lambda b, i, j: (b, i, 0))
    w_spec = pl.BlockSpec((K, tn), lambda b, i, j: (0, j))
    o_spec = pl.BlockSpec((None, tm, tn), lambda b, i, j: (b, i, j))
    in_specs, args = [a_spec, w_spec], [a, w]
    if mode == "plain":
        body = _mm_plain_body
    elif mode == "res":
        body, out_dtype = _mm_res_body, F32
        in_specs += [o_spec, pl.BlockSpec((None, 1, tn), lambda b, i, j: (b, 0, j))]
        args += [res, gate]
    elif mode == "rope":
        assert n_rope_cols % tn == 0
        body = functools.partial(_mm_rope_body, n_rope=n_rope_cols // tn)
        tab = pl.BlockSpec((tm, HEAD_DIM), lambda b, i, j: (i, 0))
        in_specs += [tab, tab, tab]
        args += list(rope)
    elif mode == "conv3":
        body, out_dtype = _mm_conv3_body, F32
        in_specs += [pl.BlockSpec((3, tn), lambda b, i, j: (0, j)),
                     pl.BlockSpec((1, tn), lambda b, i, j: (0, j))]
        args += [conv_w, conv_b.reshape(1, N)]
    else:
        raise ValueError(mode)
    return pl.pallas_call(
        body, grid=grid, in_specs=in_specs, out_specs=o_spec,
        out_shape=jax.ShapeDtypeStruct((B, L, N), out_dtype),
        compiler_params=_params("parallel", "parallel", "arbitrary"),
        name="mm_" + mode,
    )(*args)


def _attn_body(sink_ref, q_ref, *refs, nkv, grp, nb, windowed):
    if windowed:
        kp_ref, kc_ref, kn_ref, kx_ref, o_ref = refs
    else:
        kx_ref, o_ref = refs
    n = pl.program_id(1)
    scale = HEAD_DIM ** -0.5
    rows = grp * ATTN_BLOCK
    a = lax.broadcasted_iota(jnp.int32, (rows, ATTN_BLOCK), 0) % ATTN_BLOCK
    kb = lax.broadcasted_iota(jnp.int32, (rows, ATTN_BLOCK), 1)

    def head_cols(ref, c):
        return ref[:, c * HEAD_DIM:(c + 1) * HEAD_DIM]

    for h in range(nkv):
        qh = jnp.concatenate([head_cols(q_ref, h * grp + g) for g in range(grp)], axis=0)
        sink = jnp.concatenate(
            [jnp.full((ATTN_BLOCK, 1), sink_ref[h * grp + g], F32) for g in range(grp)], axis=0)

        def scores(ref):
            return lax.dot_general(qh, head_cols(ref, h), NT_DIMS, preferred_element_type=F32) * scale

        pieces = []
        if windowed:
            s_p = jnp.where(jnp.logical_and(kb >= a, n > 0), scores(kp_ref), NEG_INF)
            s_n = jnp.where(jnp.logical_and(kb <= a, n < nb - 1), scores(kn_ref), NEG_INF)
            pieces += [(s_p, kp_ref), (scores(kc_ref), kc_ref), (s_n, kn_ref)]
        pieces.append((scores(kx_ref), kx_ref))

        m = sink
        for s, _ in pieces:
            m = jnp.maximum(m, jnp.max(s, axis=-1, keepdims=True))
        den = jnp.exp(sink - m)
        o = jnp.zeros((rows, HEAD_DIM), F32)
        for s, ref in pieces:
            p = jnp.exp(s - m)
            den = den + jnp.sum(p, axis=-1, keepdims=True)
            o = o + jnp.dot(p.astype(BF16), head_cols(ref, nkv + h), preferred_element_type=F32)
        o = o / den
        for g in range(grp):
            c = h * grp + g
            o_ref[:, c * HEAD_DIM:(c + 1) * HEAD_DIM] = (
                o[g * ATTN_BLOCK:(g + 1) * ATTN_BLOCK].astype(o_ref.dtype))


def attention(qkv, qkv_ctx, sink, *, nq, nkv, windowed):
    B, L, _ = qkv.shape
    C = qkv_ctx.shape[1]
    nb = L // ATTN_BLOCK
    qd, kvd = nq * HEAD_DIM, nkv * HEAD_DIM
    assert qd % (2 * kvd) == 0
    kv_col = qd // (2 * kvd)
    q_spec = pl.BlockSpec((None, ATTN_BLOCK, qd), lambda b, n, s: (b, n, 0))

    def kv_spec(off):
        return pl.BlockSpec(
            (None, ATTN_BLOCK, 2 * kvd),
            lambda b, n, s: (b, jnp.clip(n + off, 0, nb - 1), kv_col))

    ctx_spec = pl.BlockSpec((None, C, 2 * kvd), lambda b, n, s: (b, 0, kv_col))
    if windowed:
        in_specs = [q_spec, kv_spec(-1), kv_spec(0), kv_spec(1), ctx_spec]
        args = [qkv, qkv, qkv, qkv, qkv_ctx]
    else:
        in_specs, args = [q_spec, ctx_spec], [qkv, qkv_ctx]
    body = functools.partial(_attn_body, nkv=nkv, grp=nq // nkv, nb=nb, windowed=windowed)
    return pl.pallas_call(
        body,
        grid_spec=pltpu.PrefetchScalarGridSpec(
            num_scalar_prefetch=1, grid=(B, nb), in_specs=in_specs,
            out_specs=pl.BlockSpec((None, ATTN_BLOCK, qd), lambda b, n, s: (b, n, 0))),
        out_shape=jax.ShapeDtypeStruct((B, L, qd), BF16),
        compiler_params=_params("parallel", "parallel"),
        name="attn_win" if windowed else "attn_ctx",
    )(sink.astype(F32), *args)


def rope_tables(L):
    rows = L // GRID_W
    row = jnp.repeat(jnp.arange(rows), GRID_W).astype(F32)
    col = jnp.tile(jnp.arange(GRID_W), rows).astype(F32)
    n_freq = HEAD_DIM // 4
    inv = ROPE_BASE ** (-jnp.arange(n_freq, dtype=F32) / n_freq)
    ar, ac = row[:, None] * inv, col[:, None] * inv
    ang = jnp.concatenate([ar, ar, ac, ac], axis=-1)
    cos, sin = jnp.cos(ang), jnp.sin(ang)
    odd = (jnp.arange(HEAD_DIM) // n_freq) % 2 == 1
    return cos, jnp.where(odd, sin, 0.0), jnp.where(odd, 0.0, -sin)


@functools.lru_cache(maxsize=None)
def _dft_matrices(L):
    N = 2 * L
    fb = min(256, L)
    nfb = L // fb
    k = np.arange(L, dtype=np.int64)[:, None]
    t = np.arange(L, dtype=np.int64)[None, :]
    ang = 2.0 * np.pi * ((k * t) % N).astype(np.float64) / N
    c, s = np.cos(ang), np.sin(ang)
    nyq = np.where(np.arange(L) % 2 == 0, 1.0, -1.0)
    f_re, f_im = c.copy(), -s
    f_im[0] = nyq
    i_re, i_im = 2.0 * c / N, -2.0 * s / N
    i_re[0] = 1.0 / N
    i_im[0] = nyq / N
    fwd = np.concatenate([f_re.reshape(nfb, fb, L), f_im.reshape(nfb, fb, L)], axis=1)
    inv = np.concatenate([i_re.reshape(nfb, fb, L), i_im.reshape(nfb, fb, L)], axis=1)
    inv = np.transpose(inv, (0, 2, 1))
    return fwd.astype(np.float32), inv.astype(np.float32), fb


def _filter_body(z_ref, w1_ref, b1_ref, w2_ref, b2_ref, w3_ref, dec_ref, o_ref, *, n_inner):
    hi = lax.Precision.HIGHEST
    h = jnp.sin(jnp.dot(z_ref[...], w1_ref[...], precision=hi, preferred_element_type=F32)
                + b1_ref[...])
    for i in range(n_inner):
        h = jnp.sin(jnp.dot(h, w2_ref[i], precision=hi, preferred_element_type=F32)
                    + b2_ref[i])
    y = jnp.dot(h, w3_ref[...], precision=hi, preferred_element_type=F32)
    L = y.shape[0]
    t01 = lax.broadcasted_iota(jnp.int32, y.shape, 0).astype(F32) / L
    o_ref[...] = y * (jnp.exp(-t01 * jnp.abs(dec_ref[...])) + HY_MOD_SHIFT)


def hyena_filters(L, f_w1, f_b1, f_w2, f_b2, f_w3, decay):
    order, D = decay.shape
    hid = f_w1.shape[1]
    n_inner = f_w2.shape[0]
    n_dir = f_w3.shape[1] // (order * D)
    t = np.arange(L, dtype=np.float32)
    bands = np.arange(1, HY_EMB_BANDS + 1, dtype=np.float32)
    ang = (np.float32(2.0 * math.pi) * t[:, None] * bands[None, :] / np.float32(L)).astype(np.float32)
    z = np.concatenate([(t / np.float32(L))[:, None], np.cos(ang), np.sin(ang)], axis=-1)
    emb = z.shape[1]
    emb_pad = -(-emb // 8) * 8
    z = np.pad(z, ((0, 0), (0, emb_pad - emb))).astype(np.float32)
    w1 = jnp.pad(f_w1, ((0, emb_pad - emb), (0, 0)))
    tn = _tile(D, 1024)
    per_o = n_dir * D // tn
    full = lambda shape: pl.BlockSpec(shape, lambda j: (0,) * len(shape))
    out = pl.pallas_call(
        functools.partial(_filter_body, n_inner=n_inner),
        grid=(order * n_dir * D // tn,),
        in_specs=[full((L, emb_pad)), full((emb_pad, hid)), full((1, hid)),
                  full((n_inner, hid, hid)), full((n_inner, 1, hid)),
                  pl.BlockSpec((hid, tn), lambda j: (0, j)),
                  pl.BlockSpec((None, 1, tn), lambda j: (j // per_o, 0, j % (D // tn)))],
        out_specs=pl.BlockSpec((L, tn), lambda j: (0, j)),
        out_shape=jax.ShapeDtypeStruct((L, order * n_dir * D), F32),
        compiler_params=_params("parallel"),
        name="hyena_filter",
    )(jnp.asarray(z), w1, f_b1.reshape(1, hid), f_w2, f_b2.reshape(n_inner, 1, hid), f_w3,
      decay.reshape(order, 1, D))
    return out.reshape(L, order, n_dir, D)


def hyena_spectra(L, fwd, filt):
    order, D = filt.shape[1], filt.shape[3]
    nfb, fb2, _ = fwd.shape
    fb = fb2 // 2
    f = filt[:, :, 0]
    g = filt[:, :, 1].at[0].set(0.0)
    cols = jnp.stack([f + g, f - g], axis=2).reshape(L, order * 2 * D)
    hi = cols.astype(BF16)
    lo = (cols - hi.astype(F32)).astype(BF16)
    spec = matmul(fwd.reshape(1, nfb * fb2, L), jnp.concatenate([hi, lo], axis=1))
    spec = spec.reshape(nfb, fb2, 2, order, 2, D)
    spec = spec[:, :, 0] + spec[:, :, 1]
    planes = []
    for o in range(order):
        a = spec[:, :fb, o, 0]
        b = spec[:, fb:, o, 1].at[0, 0].set(0.0)
        c = a.at[0, 0].set(spec[0, fb, o, 0])
        planes.append((a, b, c))
    return planes


def _lconv_body(v_ref, x_ref, fw_ref, iv_ref, ha_ref, hb_ref, hc_ref, bias_ref, o_ref,
                acc_ref, vb_ref, *, fb):
    f = pl.program_id(2)

    @pl.when(f == 0)
    def _():
        vb_ref[...] = v_ref[...].astype(BF16)
        acc_ref[...] = jnp.zeros_like(acc_ref)

    spec = jnp.dot(fw_ref[...], vb_ref[...], preferred_element_type=F32)
    xr, xi = spec[:fb], spec[fb:]
    a, b, c = ha_ref[...], hb_ref[...], hc_ref[...]
    yr = (xr * a - xi * b).astype(BF16)
    yi = (xr * b + xi * c).astype(BF16)
    acc_ref[...] += (jnp.dot(iv_ref[:, :fb], yr, preferred_element_type=F32)
                     + jnp.dot(iv_ref[:, fb:], yi, preferred_element_type=F32))

    @pl.when(f == pl.num_programs(2) - 1)
    def _():
        o_ref[...] = (x_ref[...] * (acc_ref[...] + v_ref[...] * bias_ref[...])).astype(o_ref.dtype)


def long_conv_gated(src, src_blk, gate, gate_blk, fwd, inv, planes, bias, out_dtype):
    B, L, _ = src.shape
    D = bias.shape[0]
    nfb, fb2, _ = fwd.shape
    fb = fb2 // 2
    tc = _tile(D, 512)
    ncb = D // tc
    ha, hb, hc = planes
    col = lambda blk: pl.BlockSpec((None, L, tc), lambda c, b, f: (b, 0, blk * ncb + c))
    hspec = pl.BlockSpec((None, fb, tc), lambda c, b, f: (f, 0, c))
    return pl.pallas_call(
        functools.partial(_lconv_body, fb=fb),
        grid=(ncb, B, nfb),
        in_specs=[col(src_blk), col(gate_blk),
                  pl.BlockSpec((None, fb2, L), lambda c, b, f: (f, 0, 0)),
                  pl.BlockSpec((None, L, fb2), lambda c, b, f: (f, 0, 0)),
                  hspec, hspec, hspec,
                  pl.BlockSpec((1, tc), lambda c, b, f: (0, c))],
        out_specs=pl.BlockSpec((None, L, tc), lambda c, b, f: (b, 0, c)),
        out_shape=jax.ShapeDtypeStruct((B, L, D), out_dtype),
        scratch_shapes=[pltpu.VMEM((L, tc), F32), pltpu.VMEM((L, tc), BF16)],
        compiler_params=_params("parallel", "parallel", "arbitrary"),
        name="long_conv",
    )(src, gate, fwd, inv, ha, hb, hc, bias.reshape(1, D))


def hyena_mix(a, res, gate, w_in, conv_w, conv_b, w_out, fbias, dft, planes):
    fwd, inv = dft
    u = matmul(a, w_in, mode="conv3", conv_w=conv_w, conv_b=conv_b)
    z = long_conv_gated(u, 0, u, 1, fwd, inv, planes[0], fbias[0], F32)
    y = long_conv_gated(z, 0, u, 2, fwd, inv, planes[1], fbias[1], BF16)
    return matmul(y, w_out, mode="res", res=res, gate=gate)


def _topk_rows(s, k, key=None):
    if key is None:
        key = lax.broadcasted_iota(jnp.int32, s.shape, 0)
    big = jnp.iinfo(jnp.int32).max
    vals, keys = [], []
    for _ in range(k):
        m = jnp.max(s, axis=0, keepdims=True)
        i = jnp.min(jnp.where(s == m, key, big), axis=0, keepdims=True)
        vals.append(m)
        keys.append(i)
        s = jnp.where(key == i, -jnp.inf, s)
    return vals, keys


def _route_body(q_ref, k1_ref, k2_ref, idx_ref, g_ref, *, half, n_keys):
    T = q_ref.shape[0]
    K = PEER_TOPK
    assert K == 16
    r8 = lax.broadcasted_iota(jnp.int32, (8, LANES), 0)
    r16 = lax.broadcasted_iota(jnp.int32, (K, LANES), 0)
    for c in range(T // LANES):
        q = q_ref[c * LANES:(c + 1) * LANES, :]
        s1 = lax.dot_general(k1_ref[...], q[:, :half], NT_DIMS, preferred_element_type=F32)
        s2 = lax.dot_general(k2_ref[...], q[:, half:], NT_DIMS, preferred_element_type=F32)
        v1, i1 = _topk_rows(s1, K)
        v2, i2 = _topk_rows(s2, K)
        v1c, i1c = jnp.concatenate(v1, axis=0), jnp.concatenate(i1, axis=0)
        v2c, i2c = jnp.concatenate(v2, axis=0), jnp.concatenate(i2, axis=0)
        cand = [v1[0] + v2c]
        cid = [i1[0] * n_keys + i2c]
        pos = [r16]
        for i in range(1, 8):
            n = K // (i + 1)
            val = v1[i] + v2c[:8]
            cand.append(val if n >= 8 else jnp.where(r8 < n, val, -jnp.inf))
            cid.append(i1[i] * n_keys + i2c[:8])
            pos.append(i * K + r8)
        cand.append(v1c[8:] + v2[0])
        cid.append(i1c[8:] * n_keys + i2[0])
        pos.append((8 + r8) * K)
        cand, cid, pos = (jnp.concatenate(x, axis=0) for x in (cand, cid, pos))
        top, sel = _topk_rows(cand, K, key=pos)
        ids = [jnp.max(jnp.where(pos == p, cid, -1), axis=0, keepdims=True) for p in sel]
        e = [jnp.exp(t - top[0]) for t in top]
        den = e[0]
        for x in e[1:]:
            den = den + x
        idx_ref[:, c * LANES:(c + 1) * LANES] = jnp.concatenate(ids, axis=0)
        g_ref[:, c * LANES:(c + 1) * LANES] = jnp.concatenate(e, axis=0) / den


def peer_route(q, keys1, keys2):
    M = q.shape[0]
    H, n_keys, half = keys1.shape
    T = _tile(M, 256)
    kspec = pl.BlockSpec((None, n_keys, half), lambda i, h: (h, 0, 0))
    ospec = pl.BlockSpec((None, PEER_TOPK, T), lambda i, h: (h, 0, i))
    return pl.pallas_call(
        functools.partial(_route_body, half=half, n_keys=n_keys),
        grid=(M // T, H),
        in_specs=[pl.BlockSpec((T, 2 * half), lambda i, h: (i, h)), kspec, kspec],
        out_specs=[ospec, ospec],
        out_shape=[jax.ShapeDtypeStruct((H, PEER_TOPK, M), jnp.int32),
                   jax.ShapeDtypeStruct((H, PEER_TOPK, M), F32)],
        compiler_params=_params("parallel", "arbitrary"),
        name="peer_route",
    )(q, keys1.astype(BF16), keys2.astype(BF16))


GATHER_GROUP = 4
GATHER_RING = 4
GATHER_SLOTS = GATHER_GROUP * GATHER_RING
CHUNK_ROWS = 4


def pack_expert_tables(u, v):
    E, D = u.shape
    ub = lax.bitcast_convert_type(u.astype(jnp.bfloat16), jnp.uint16).astype(jnp.uint32)
    vb = lax.bitcast_convert_type(v.astype(jnp.bfloat16), jnp.uint16).astype(jnp.uint32)
    return (ub | (vb << 16)).reshape(E, D // (CHUNK_ROWS * LANES), CHUNK_ROWS, LANES)


def _gather_body(idx_hbm, wg_ref, x_ref, res_ref, g2_ref, uv_hbm, o_ref, idx_smem, *scratch,
                 n_sel, d):
    slots, (isem, sems) = scratch[:GATHER_SLOTS], scratch[GATHER_SLOTS:]
    sub = d // LANES
    tb = x_ref.shape[0]
    n_idx = tb * n_sel
    step = pl.program_id(0)
    last_step = pl.num_programs(0) - 1
    cur = (step % 2) * n_idx
    nxt = n_idx - cur

    def idx_copy(blk, off):
        return pltpu.make_async_copy(idx_hbm.at[blk], idx_smem.at[pl.ds(off, n_idx)], isem)

    def row_copy(e, j, s):
        return pltpu.make_async_copy(
            uv_hbm.at[e], slots[s].at[:, pl.ds(j * CHUNK_ROWS, CHUNK_ROWS), :], sems.at[s])

    def issue_group(g, ring):
        for k in range(GATHER_GROUP):
            t = g * GATHER_GROUP + k
            beyond = jnp.where(step == last_step, cur + (tb - 1) * n_sel, nxt + (t - tb) * n_sel)
            base = jnp.where(t < tb, cur + t * n_sel, beyond)
            for j in range(n_sel):
                row_copy(idx_smem[base + j], j, ring * GATHER_GROUP + k).start(priority=j % 2)

    def wait_group(ring):
        for k in range(GATHER_GROUP):
            for j in range(n_sel):
                row_copy(0, j, ring * GATHER_GROUP + k).wait()

    ones = jnp.ones((8, LANES), BF16)

    def compute(t, s):
        x = x_ref[pl.ds(t, 1), :]
        p, vs = None, []
        for q in range(sub):
            word = slots[s][q // CHUNK_ROWS, pl.ds(q % CHUNK_ROWS, n_sel, stride=CHUNK_ROWS), :]
            u_q = lax.bitcast_convert_type(word << 16, F32)
            vs.append(lax.bitcast_convert_type(word & jnp.uint32(0xFFFF0000), F32))
            term = u_q * x[:, q * LANES:(q + 1) * LANES]
            p = term if p is None else p + term
        hi = p.astype(BF16)
        lo = (p - hi.astype(F32)).astype(BF16)
        act = (lax.dot_general(ones, hi, NT_DIMS, preferred_element_type=F32)
               + lax.dot_general(ones, lo, NT_DIMS, preferred_element_type=F32))
        gelu = 0.5 * act * (1.0 + lax.erf(act * (2.0 ** -0.5)))
        w = (wg_ref[pl.ds(t, 1), :] * gelu).astype(BF16)
        v = jnp.concatenate(vs, axis=1).astype(BF16)
        out = jnp.dot(w, v, preferred_element_type=F32)
        o_ref[pl.ds(t, 1), :] = res_ref[pl.ds(t, 1), :] + g2_ref[...] * out[0:1]

    @pl.when(step == 0)
    def _():
        first = idx_copy(0, 0)
        first.start()
        first.wait()
        for r in range(GATHER_RING - 1):
            issue_group(r, r)

    @pl.when(step < last_step)
    def _():
        idx_copy(step + 1, nxt).start()

    n_sweeps = tb // GATHER_SLOTS

    def sweep(it, carry):
        @pl.when(jnp.logical_and(it == n_sweeps - 1, step < last_step))
        def _():
            idx_copy(step + 1, nxt).wait()

        for r in range(GATHER_RING):
            g = it * GATHER_RING + r
            wait_group(r)
            issue_group(g + GATHER_RING - 1, (r - 1) % GATHER_RING)
            for k in range(GATHER_GROUP):
                compute(g * GATHER_GROUP + k, r * GATHER_GROUP + k)
        return carry

    lax.fori_loop(0, n_sweeps, sweep, 0)

    @pl.when(step == last_step)
    def _():
        for r in range(GATHER_RING - 1):
            wait_group(r)


def peer_experts(idx, gates, x, res, gate2, uv):
    B, L, D = x.shape
    M = B * L
    n_sel = idx.shape[1]
    tb = _tile(L, 512)
    lb = L // tb
    assert tb % GATHER_SLOTS == 0
    body = functools.partial(_gather_body, n_sel=n_sel, d=D)
    tok = lambda shape_last: pl.BlockSpec((tb, shape_last), lambda i: (i, 0))
    out = pl.pallas_call(
        body,
        grid=(M // tb,),
        in_specs=[pl.BlockSpec(memory_space=pl.ANY), tok(n_sel), tok(D), tok(D),
                  pl.BlockSpec((None, 1, D), lambda i: (i // lb, 0, 0)),
                  pl.BlockSpec(memory_space=pl.ANY)],
        out_specs=tok(D),
        out_shape=jax.ShapeDtypeStruct((M, D), F32),
        scratch_shapes=([pltpu.SMEM((2 * tb * n_sel,), jnp.int32)]
                        + [pltpu.VMEM((D // (CHUNK_ROWS * LANES), n_sel * CHUNK_ROWS, LANES),
                                      jnp.uint32)] * GATHER_SLOTS
                        + [pltpu.SemaphoreType.DMA(()), pltpu.SemaphoreType.DMA((GATHER_SLOTS,))]),
        compiler_params=_params("arbitrary"),
        name="peer_experts",
    )(idx.reshape(M // tb, tb * n_sel), gates, x.reshape(M, D), res.reshape(M, D), gate2, uv)
    return out.reshape(B, L, D)


def peer_mix(h, g, shift, scale, gate2, w_q, keys1, keys2, uv):
    B, L, D = h.shape
    a, a32 = modulate(h, g, shift, scale, out_dtypes=(BF16, F32))
    q = matmul(a, w_q, out_dtype=BF16)
    idx, gates = peer_route(q.reshape(B * L, -1), keys1, keys2)
    n_sel = idx.shape[0] * idx.shape[1]
    idx = idx.reshape(n_sel, B * L).T
    gates = gates.reshape(n_sel, B * L).T
    return peer_experts(idx, gates, a32, h, gate2, uv)


def kernel(x, c, ctx, c_ctx, ada_w, ada_b, norm_mix_g, norm_ffn_g, final_g, attn_w_qkv, attn_w_o, attn_sink, hy_w_in, hy_conv_w, hy_conv_b, hy_f_w1, hy_f_b1, hy_f_w2, hy_f_b2, hy_f_w3, hy_decay, hy_fbias, hy_w_out, peer_w_q, peer_keys1, peer_keys2, peer_u, peer_v):
    B, L, D = x.shape
    C = ctx.shape[1]
    depth = ada_w.shape[0]
    n_mixers = 2
    last_attn = max(i for i in range(depth) if i % n_mixers == 0)
    nq = attn_w_o.shape[1] // HEAD_DIM
    nkv = (attn_w_qkv.shape[2] - nq * HEAD_DIM) // (2 * HEAD_DIM)

    cond = jnp.concatenate([jax.nn.silu(c), jax.nn.silu(c_ctx)[None]], axis=0)
    cond = jnp.pad(cond, ((0, -(B + 1) % 16), (0, 0))).astype(BF16)[None]
    rope = rope_tables(L)
    h_lat, h_ctx = x, ctx

    for i in range(depth):
        is_attn = i % n_mixers == 0
        j = i // n_mixers
        ctx_update = i < last_attn
        mods = matmul(cond, ada_w[i])[0, :B + 1] + ada_b[i]
        lat = [m[:, None, :] for m in jnp.split(mods[:B], N_MODS, axis=-1)]
        cx = [jnp.broadcast_to(m[None], (B, 1, D)) for m in jnp.split(mods[B:], N_MODS, axis=-1)]
        sh1, sc1, g1, sh2, sc2, g2 = lat
        csh1, csc1, cg1, csh2, csc2, cg2 = cx
        a_lat = modulate(h_lat, norm_mix_g[i], sh1, sc1)

        if is_attn:
            w_qkv = attn_w_qkv[j].astype(BF16)
            w_o = attn_w_o[j].astype(BF16)
            a_ctx = modulate(h_ctx, norm_mix_g[i], csh1, csc1)
            qkv = matmul(a_lat, w_qkv, mode="rope", out_dtype=BF16, rope=rope,
                         n_rope_cols=(nq + nkv) * HEAD_DIM)
            qkv_ctx = matmul(a_ctx, w_qkv, out_dtype=BF16)
            o_lat = attention(qkv, qkv_ctx, attn_sink[j], nq=nq, nkv=nkv, windowed=True)
            h_lat = matmul(o_lat, w_o, mode="res", res=h_lat, gate=g1)
            if ctx_update:
                o_ctx = attention(qkv_ctx, qkv_ctx, attn_sink[j], nq=nq, nkv=nkv, windowed=False)
                h_ctx = matmul(o_ctx, w_o, mode="res", res=h_ctx, gate=cg1)
        else:
            w_in = hy_w_in[j].astype(BF16)
            w_out = hy_w_out[j].astype(BF16)
            f_args = (hy_f_w1[j], hy_f_b1[j], hy_f_w2[j], hy_f_b2[j], hy_f_w3[j], hy_decay[j])
            seqs = [(a_lat, h_lat, g1)]
            if ctx_update:
                seqs.append((modulate(h_ctx, norm_mix_g[i], csh1, csc1), h_ctx, cg1))
            outs = []
            for a_seq, h_seq, gate in seqs:
                Ls = a_seq.shape[1]
                fwd, inv, _ = _dft_matrices(Ls)
                fwd, inv = jnp.asarray(fwd, BF16), jnp.asarray(inv, BF16)
                planes = hyena_spectra(Ls, fwd, hyena_filters(Ls, *f_args))
                outs.append(hyena_mix(a_seq, h_seq, gate, w_in, hy_conv_w[j], hy_conv_b[j],
                                      w_out, hy_fbias[j], (fwd, inv), planes))
            h_lat = outs[0]
            if ctx_update:
                h_ctx = outs[1]

        w_q = peer_w_q[i].astype(BF16)
        uv = pack_expert_tables(peer_u[i], peer_v[i])
        h_lat = peer_mix(h_lat, norm_ffn_g[i], sh2, sc2, g2, w_q, peer_keys1[i], peer_keys2[i], uv)
        if ctx_update:
            h_ctx = peer_mix(h_ctx, norm_ffn_g[i], csh2, csc2, cg2, w_q, peer_keys1[i],
                             peer_keys2[i], uv)

    zero = jnp.zeros((B, 1, D), F32)
    return modulate(h_lat, final_g, zero, zero, out_dtypes=(F32,))
```

```python
import functools
import math

import numpy as np
import jax
import jax.numpy as jnp
from jax import lax
from jax.experimental import pallas as pl
from jax.experimental.pallas import tpu as pltpu

F32 = jnp.float32
BF16 = jnp.bfloat16

LANES = 128
HEAD_DIM = 128
GRID_W = 64
ATTN_BLOCK = 128
ROPE_BASE = 10000.0
RMS_EPS = 1e-6
NEG_INF = -1e30
N_MODS = 6
HY_EMB_BANDS = 16
HY_MOD_SHIFT = 0.05
PEER_TOPK = 16
VMEM_LIMIT = 56 * 2 ** 20
NT_DIMS = (((1,), (1,)), ((), ()))


def _params(*sem):
    return pltpu.CompilerParams(dimension_semantics=sem, vmem_limit_bytes=VMEM_LIMIT)


def _tile(n, pref):
    if n <= pref:
        return n
    t = (pref // LANES) * LANES
    while n % t:
        t -= LANES
    return t


def _modulate_body(h_ref, g_ref, sh_ref, sc_ref, *o_refs):
    x = h_ref[...]
    ms = jnp.mean(x * x, axis=-1, keepdims=True)
    y = x * lax.rsqrt(ms + RMS_EPS) * g_ref[...]
    y = y * (1.0 + sc_ref[...]) + sh_ref[...]
    for o_ref in o_refs:
        o_ref[...] = y.astype(o_ref.dtype)


def modulate(h, g, shift, scale, out_dtypes=None):
    B, L, D = h.shape
    out_dtypes = out_dtypes or (BF16,)
    tm = _tile(L, 512)
    vec = pl.BlockSpec((None, 1, D), lambda b, i: (b, 0, 0))
    blk = pl.BlockSpec((None, tm, D), lambda b, i: (b, i, 0))
    outs = pl.pallas_call(
        _modulate_body,
        grid=(B, L // tm),
        in_specs=[blk, pl.BlockSpec((1, D), lambda b, i: (0, 0)), vec, vec],
        out_specs=[blk] * len(out_dtypes),
        out_shape=[jax.ShapeDtypeStruct((B, L, D), dt) for dt in out_dtypes],
        compiler_params=_params("parallel", "parallel"),
        name="modulate",
    )(h, g.reshape(1, D), shift, scale)
    return outs[0] if len(outs) == 1 else outs


def _mm_plain_body(a_ref, w_ref, o_ref):
    acc = jnp.dot(a_ref[...], w_ref[...].astype(BF16), preferred_element_type=F32)
    o_ref[...] = acc.astype(o_ref.dtype)


def _mm_res_body(a_ref, w_ref, r_ref, g_ref, o_ref):
    acc = jnp.dot(a_ref[...], w_ref[...].astype(BF16), preferred_element_type=F32)
    o_ref[...] = r_ref[...] + g_ref[...] * acc


def _mm_rope_body(a_ref, w_ref, cos_ref, sa_ref, sb_ref, o_ref, *, n_rope):
    acc = jnp.dot(a_ref[...], w_ref[...].astype(BF16), preferred_element_type=F32)
    j = pl.program_id(2)

    @pl.when(j < n_rope)
    def _():
        cos, sa, sb = cos_ref[...], sa_ref[...], sb_ref[...]
        quarter = HEAD_DIM // 4
        for s in range(acc.shape[1] // HEAD_DIM):
            x = acc[:, s * HEAD_DIM:(s + 1) * HEAD_DIM]
            y = (x * cos + pltpu.roll(x, quarter, axis=1) * sa
                 + pltpu.roll(x, HEAD_DIM - quarter, axis=1) * sb)
            o_ref[:, s * HEAD_DIM:(s + 1) * HEAD_DIM] = y.astype(o_ref.dtype)

    @pl.when(j >= n_rope)
    def _():
        o_ref[...] = acc.astype(o_ref.dtype)


def _mm_conv3_body(a_ref, w_ref, cw_ref, cb_ref, o_ref):
    y = jnp.dot(a_ref[...], w_ref[...].astype(BF16), preferred_element_type=F32)
    L = y.shape[0]
    row = lax.broadcasted_iota(jnp.int32, y.shape, 0)
    prev = jnp.where(row == 0, 0.0, pltpu.roll(y, 1, axis=0))
    nxt = jnp.where(row == L - 1, 0.0, pltpu.roll(y, L - 1, axis=0))
    o_ref[...] = cb_ref[...] + prev * cw_ref[0:1, :] + y * cw_ref[1:2, :] + nxt * cw_ref[2:3, :]


def matmul(a, w, *, mode="plain", out_dtype=F32, tm=2048, tn=512, res=None, gate=None,
           rope=None, n_rope_cols=0, conv_w=None, conv_b=None):
    B, L, K = a.shape
    N = w.shape[1]
    tm = L if mode == "conv3" else _tile(L, tm)
    if mode == "rope":
        tn = _tile(math.gcd(N, n_rope_cols), tn)
    else:
        tn = _tile(N, 256 if mode == "conv3" else tn)
    grid = (B, L // tm, N // tn)
    a_spec = pl.BlockSpec((None, tm, K), lambda b, i, j: (b, i, 0))
    w_spec = pl.BlockSpec((K, tn), lambda b, i, j: (0, j))
    o_spec = pl.BlockSpec((None, tm, tn), lambda b, i, j: (b, i, j))
    in_specs, args = [a_spec, w_spec], [a, w]
    if mode == "plain":
        body = _mm_plain_body
    elif mode == "res":
        body, out_dtype = _mm_res_body, F32
        in_specs += [o_spec, pl.BlockSpec((None, 1, tn), lambda b, i, j: (b, 0, j))]
        args += [res, gate]
    elif mode == "rope":
        assert n_rope_cols % tn == 0
        body = functools.partial(_mm_rope_body, n_rope=n_rope_cols // tn)
        tab = pl.BlockSpec((tm, HEAD_DIM), lambda b, i, j: (i, 0))
        in_specs += [tab, tab, tab]
        args += list(rope)
    elif mode == "conv3":
        body, out_dtype = _mm_conv3_body, F32
        in_specs += [pl.BlockSpec((3, tn), lambda b, i, j: (0, j)),
                     pl.BlockSpec((1, tn), lambda b, i, j: (0, j))]
        args += [conv_w, conv_b.reshape(1, N)]
    else:
        raise ValueError(mode)
    return pl.pallas_call(
        body, grid=grid, in_specs=in_specs, out_specs=o_spec,
        out_shape=jax.ShapeDtypeStruct((B, L, N), out_dtype),
        compiler_params=_params("parallel", "parallel", "arbitrary"),
        name="mm_" + mode,
    )(*args)


def _attn_body(sink_ref, q_ref, *refs, nkv, grp, nb, windowed):
    if windowed:
        kp_ref, kc_ref, kn_ref, kx_ref, o_ref = refs
    else:
        kx_ref, o_ref = refs
    n = pl.program_id(1)
    scale = HEAD_DIM ** -0.5
    rows = grp * ATTN_BLOCK
    a = lax.broadcasted_iota(jnp.int32, (rows, ATTN_BLOCK), 0) % ATTN_BLOCK
    kb = lax.broadcasted_iota(jnp.int32, (rows, ATTN_BLOCK), 1)

    def head_cols(ref, c):
        return ref[:, c * HEAD_DIM:(c + 1) * HEAD_DIM]

    for h in range(nkv):
        qh = jnp.concatenate([head_cols(q_ref, h * grp + g) for g in range(grp)], axis=0)
        sink = jnp.concatenate(
            [jnp.full((ATTN_BLOCK, 1), sink_ref[h * grp + g], F32) for g in range(grp)], axis=0)

        def scores(ref):
            return lax.dot_general(qh, head_cols(ref, h), NT_DIMS, preferred_element_type=F32) * scale

        pieces = []
        if windowed:
            s_p = jnp.where(jnp.logical_and(kb >= a, n > 0), scores(kp_ref), NEG_INF)
            s_n = jnp.where(jnp.logical_and(kb <= a, n < nb - 1), scores(kn_ref), NEG_INF)
            pieces += [(s_p, kp_ref), (scores(kc_ref), kc_ref), (s_n, kn_ref)]
        pieces.append((scores(kx_ref), kx_ref))

        m = sink
        for s, _ in pieces:
            m = jnp.maximum(m, jnp.max(s, axis=-1, keepdims=True))
        den = jnp.exp(sink - m)
        o = jnp.zeros((rows, HEAD_DIM), F32)
        for s, ref in pieces:
            p = jnp.exp(s - m)
            den = den + jnp.sum(p, axis=-1, keepdims=True)
            o = o + jnp.dot(p.astype(BF16), head_cols(ref, nkv + h), preferred_element_type=F32)
        o = o / den
        for g in range(grp):
            c = h * grp + g
            o_ref[:, c * HEAD_DIM:(c + 1) * HEAD_DIM] = (
                o[g * ATTN_BLOCK:(g + 1) * ATTN_BLOCK].astype(o_ref.dtype))


def attention(qkv, qkv_ctx, sink, *, nq, nkv, windowed):
    B, L, _ = qkv.shape
    C = qkv_ctx.shape[1]
    nb = L // ATTN_BLOCK
    qd, kvd = nq * HEAD_DIM, nkv * HEAD_DIM
    assert qd % (2 * kvd) == 0
    kv_col = qd // (2 * kvd)
    q_spec = pl.BlockSpec((None, ATTN_BLOCK, qd), lambda b, n, s: (b, n, 0))

    def kv_spec(off):
        return pl.BlockSpec(
            (None, ATTN_BLOCK, 2 * kvd),
            lambda b, n, s: (b, jnp.clip(n + off, 0, nb - 1), kv_col))

    ctx_spec = pl.BlockSpec((None, C, 2 * kvd), lambda b, n, s: (b, 0, kv_col))
    if windowed:
        in_specs = [q_spec, kv_spec(-1), kv_spec(0), kv_spec(1), ctx_spec]
        args = [qkv, qkv, qkv, qkv, qkv_ctx]
    else:
        in_specs, args = [q_spec, ctx_spec], [qkv, qkv_ctx]
    body = functools.partial(_attn_body, nkv=nkv, grp=nq // nkv, nb=nb, windowed=windowed)
    return pl.pallas_call(
        body,
        grid_spec=pltpu.PrefetchScalarGridSpec(
            num_scalar_prefetch=1, grid=(B, nb), in_specs=in_specs,
            out_specs=pl.BlockSpec((None, ATTN_BLOCK, qd), lambda b, n, s: (b, n, 0))),
        out_shape=jax.ShapeDtypeStruct((B, L, qd), BF16),
        compiler_params=_params("parallel", "parallel"),
        name="attn_win" if windowed else "attn_ctx",
    )(sink.astype(F32), *args)


def rope_tables(L):
    rows = L // GRID_W
    row = jnp.repeat(jnp.arange(rows), GRID_W).astype(F32)
    col = jnp.tile(jnp.arange(GRID_W), rows).astype(F32)
    n_freq = HEAD_DIM // 4
    inv = ROPE_BASE ** (-jnp.arange(n_freq, dtype=F32) / n_freq)
    ar, ac = row[:, None] * inv, col[:, None] * inv
    ang = jnp.concatenate([ar, ar, ac, ac], axis=-1)
    cos, sin = jnp.cos(ang), jnp.sin(ang)
    odd = (jnp.arange(HEAD_DIM) // n_freq) % 2 == 1
    return cos, jnp.where(odd, sin, 0.0), jnp.where(odd, 0.0, -sin)


@functools.lru_cache(maxsize=None)
def _dft_matrices(L):
    N = 2 * L
    fb = min(256, L)
    nfb = L // fb
    k = np.arange(L, dtype=np.int64)[:, None]
    t = np.arange(L, dtype=np.int64)[None, :]
    ang = 2.0 * np.pi * ((k * t) % N).astype(np.float64) / N
    c, s = np.cos(ang), np.sin(ang)
    nyq = np.where(np.arange(L) % 2 == 0, 1.0, -1.0)
    f_re, f_im = c.copy(), -s
    f_im[0] = nyq
    i_re, i_im = 2.0 * c / N, -2.0 * s / N
    i_re[0] = 1.0 / N
    i_im[0] = nyq / N
    fwd = np.concatenate([f_re.reshape(nfb, fb, L), f_im.reshape(nfb, fb, L)], axis=1)
    inv = np.concatenate([i_re.reshape(nfb, fb, L), i_im.reshape(nfb, fb, L)], axis=1)
    inv = np.transpose(inv, (0, 2, 1))
    return fwd.astype(np.float32), inv.astype(np.float32), fb


def _filter_body(z_ref, w1_ref, b1_ref, w2_ref, b2_ref, w3_ref, dec_ref, o_ref, *, n_inner):
    hi = lax.Precision.HIGHEST
    h = jnp.sin(jnp.dot(z_ref[...], w1_ref[...], precision=hi, preferred_element_type=F32)
                + b1_ref[...])
    for i in range(n_inner):
        h = jnp.sin(jnp.dot(h, w2_ref[i], precision=hi, preferred_element_type=F32)
                    + b2_ref[i])
    y = jnp.dot(h, w3_ref[...], precision=hi, preferred_element_type=F32)
    L = y.shape[0]
    t01 = lax.broadcasted_iota(jnp.int32, y.shape, 0).astype(F32) / L
    o_ref[...] = y * (jnp.exp(-t01 * jnp.abs(dec_ref[...])) + HY_MOD_SHIFT)


def hyena_filters(L, f_w1, f_b1, f_w2, f_b2, f_w3, decay):
    order, D = decay.shape
    hid = f_w1.shape[1]
    n_inner = f_w2.shape[0]
    n_dir = f_w3.shape[1] // (order * D)
    t = np.arange(L, dtype=np.float32)
    bands = np.arange(1, HY_EMB_BANDS + 1, dtype=np.float32)
    ang = (np.float32(2.0 * math.pi) * t[:, None] * bands[None, :] / np.float32(L)).astype(np.float32)
    z = np.concatenate([(t / np.float32(L))[:, None], np.cos(ang), np.sin(ang)], axis=-1)
    emb = z.shape[1]
    emb_pad = -(-emb // 8) * 8
    z = np.pad(z, ((0, 0), (0, emb_pad - emb))).astype(np.float32)
    w1 = jnp.pad(f_w1, ((0, emb_pad - emb), (0, 0)))
    tn = _tile(D, 1024)
    per_o = n_dir * D // tn
    full = lambda shape: pl.BlockSpec(shape, lambda j: (0,) * len(shape))
    out = pl.pallas_call(
        functools.partial(_filter_body, n_inner=n_inner),
        grid=(order * n_dir * D // tn,),
        in_specs=[full((L, emb_pad)), full((emb_pad, hid)), full((1, hid)),
                  full((n_inner, hid, hid)), full((n_inner, 1, hid)),
                  pl.BlockSpec((hid, tn), lambda j: (0, j)),
                  pl.BlockSpec((None, 1, tn), lambda j: (j // per_o, 0, j % (D // tn)))],
        out_specs=pl.BlockSpec((L, tn), lambda j: (0, j)),
        out_shape=jax.ShapeDtypeStruct((L, order * n_dir * D), F32),
        compiler_params=_params("parallel"),
        name="hyena_filter",
    )(jnp.asarray(z), w1, f_b1.reshape(1, hid), f_w2, f_b2.reshape(n_inner, 1, hid), f_w3,
      decay.reshape(order, 1, D))
    return out.reshape(L, order, n_dir, D)


def hyena_spectra(L, fwd, filt):
    order, D = filt.shape[1], filt.shape[3]
    nfb, fb2, _ = fwd.shape
    f = filt[:, :, 0]
    g = filt[:, :, 1].at[0].set(0.0)
    cols = jnp.stack([f + g, f - g], axis=2).reshape(L, order * 2 * D)
    hi = cols.astype(BF16)
    lo = (cols - hi.astype(F32)).astype(BF16)
    fwd2 = jnp.concatenate([fwd, fwd], axis=-1).reshape(1, nfb * fb2, 2 * L)
    spec = matmul(fwd2, jnp.concatenate([hi, lo], axis=0), tm=1024)
    return spec.reshape(nfb, fb2, order * 2 * D)


def _lconv_body(v_ref, x_ref, fw_ref, iv_ref, ha_ref, hb_ref, nyq_ref, bias_ref, o_ref,
                acc_ref, vb_ref, *, fb):
    f = pl.program_id(2)

    @pl.when(f == 0)
    def _():
        vb_ref[...] = v_ref[...].astype(BF16)
        acc_ref[...] = jnp.zeros_like(acc_ref)

    spec = jnp.dot(fw_ref[...], vb_ref[...], preferred_element_type=F32)
    xr, xi = spec[:fb], spec[fb:]
    a = ha_ref[...]
    dc = jnp.logical_and(lax.broadcasted_iota(jnp.int32, a.shape, 0) == 0, f == 0)
    b = jnp.where(dc, 0.0, hb_ref[...])
    c = jnp.where(dc, nyq_ref[0:1, :], a)
    yr = (xr * a - xi * b).astype(BF16)
    yi = (xr * b + xi * c).astype(BF16)
    acc_ref[...] += (jnp.dot(iv_ref[:, :fb], yr, preferred_element_type=F32)
                     + jnp.dot(iv_ref[:, fb:], yi, preferred_element_type=F32))

    @pl.when(f == pl.num_programs(2) - 1)
    def _():
        o_ref[...] = (x_ref[...] * (acc_ref[...] + v_ref[...] * bias_ref[...])).astype(o_ref.dtype)


def long_conv_gated(src, src_blk, gate, gate_blk, fwd, inv, spec, o, bias, out_dtype):
    B, L, _ = src.shape
    D = bias.shape[0]
    nfb, fb2, _ = fwd.shape
    fb = fb2 // 2
    tc = _tile(D, 512)
    ncb = D // tc
    col = lambda blk: pl.BlockSpec((None, L, tc), lambda c, b, f: (b, 0, blk * ncb + c))

    def hspec(part, grp, fixed_f=None):
        return pl.BlockSpec((None, fb, tc), lambda c, b, f: (f if fixed_f is None else fixed_f,
                                                            part, grp * ncb + c))

    return pl.pallas_call(
        functools.partial(_lconv_body, fb=fb),
        grid=(ncb, B, nfb),
        in_specs=[col(src_blk), col(gate_blk),
                  pl.BlockSpec((None, fb2, L), lambda c, b, f: (f, 0, 0)),
                  pl.BlockSpec((None, L, fb2), lambda c, b, f: (f, 0, 0)),
                  hspec(0, 2 * o), hspec(1, 2 * o + 1), hspec(1, 2 * o, fixed_f=0),
                  pl.BlockSpec((1, tc), lambda c, b, f: (0, c))],
        out_specs=pl.BlockSpec((None, L, tc), lambda c, b, f: (b, 0, c)),
        out_shape=jax.ShapeDtypeStruct((B, L, D), out_dtype),
        scratch_shapes=[pltpu.VMEM((L, tc), F32), pltpu.VMEM((L, tc), BF16)],
        compiler_params=_params("parallel", "parallel", "arbitrary"),
        name="long_conv",
    )(src, gate, fwd, inv, spec, spec, spec, bias.reshape(1, D))


def hyena_mix(a, res, gate, w_in, conv_w, conv_b, w_out, fbias, dft, spec):
    fwd, inv = dft
    u = matmul(a, w_in, mode="conv3", conv_w=conv_w, conv_b=conv_b)
    z = long_conv_gated(u, 0, u, 1, fwd, inv, spec, 0, fbias[0], F32)
    y = long_conv_gated(z, 0, u, 2, fwd, inv, spec, 1, fbias[1], BF16)
    return matmul(y, w_out, mode="res", res=res, gate=gate)


def _topk_rows(s, k, key=None):
    if key is None:
        key = lax.broadcasted_iota(jnp.int32, s.shape, 0)
    big = jnp.iinfo(jnp.int32).max
    vals, keys = [], []
    for _ in range(k):
        m = jnp.max(s, axis=0, keepdims=True)
        i = jnp.min(jnp.where(s == m, key, big), axis=0, keepdims=True)
        vals.append(m)
        keys.append(i)
        s = jnp.where(key == i, -jnp.inf, s)
    return vals, keys


def _route_body(q_ref, k1_ref, k2_ref, idx_ref, g_ref, *, half, n_keys):
    T = q_ref.shape[0]
    K = PEER_TOPK
    assert K == 16
    r8 = lax.broadcasted_iota(jnp.int32, (8, LANES), 0)
    r16 = lax.broadcasted_iota(jnp.int32, (K, LANES), 0)
    for c in range(T // LANES):
        q = q_ref[c * LANES:(c + 1) * LANES, :]
        s1 = lax.dot_general(k1_ref[...], q[:, :half], NT_DIMS, preferred_element_type=F32)
        s2 = lax.dot_general(k2_ref[...], q[:, half:], NT_DIMS, preferred_element_type=F32)
        v1, i1 = _topk_rows(s1, K)
        v2, i2 = _topk_rows(s2, K)
        v1c, i1c = jnp.concatenate(v1, axis=0), jnp.concatenate(i1, axis=0)
        v2c, i2c = jnp.concatenate(v2, axis=0), jnp.concatenate(i2, axis=0)
        cand = [v1[0] + v2c]
        cid = [i1[0] * n_keys + i2c]
        pos = [r16]
        for i in range(1, 8):
            n = K // (i + 1)
            val = v1[i] + v2c[:8]
            cand.append(val if n >= 8 else jnp.where(r8 < n, val, -jnp.inf))
            cid.append(i1[i] * n_keys + i2c[:8])
            pos.append(i * K + r8)
        cand.append(v1c[8:] + v2[0])
        cid.append(i1c[8:] * n_keys + i2[0])
        pos.append((8 + r8) * K)
        cand, cid, pos = (jnp.concatenate(x, axis=0) for x in (cand, cid, pos))
        top, sel = _topk_rows(cand, K, key=pos)
        ids = [jnp.max(jnp.where(pos == p, cid, -1), axis=0, keepdims=True) for p in sel]
        e = [jnp.exp(t - top[0]) for t in top]
        den = e[0]
        for x in e[1:]:
            den = den + x
        idx_ref[:, c * LANES:(c + 1) * LANES] = jnp.concatenate(ids, axis=0)
        g_ref[:, c * LANES:(c + 1) * LANES] = jnp.concatenate(e, axis=0) / den


def peer_route(q, keys1, keys2):
    M = q.shape[0]
    H, n_keys, half = keys1.shape
    T = _tile(M, 256)
    kspec = pl.BlockSpec((None, n_keys, half), lambda i, h: (h, 0, 0))
    ospec = pl.BlockSpec((None, PEER_TOPK, T), lambda i, h: (h, 0, i))
    return pl.pallas_call(
        functools.partial(_route_body, half=half, n_keys=n_keys),
        grid=(M // T, H),
        in_specs=[pl.BlockSpec((T, 2 * half), lambda i, h: (i, h)), kspec, kspec],
        out_specs=[ospec, ospec],
        out_shape=[jax.ShapeDtypeStruct((H, PEER_TOPK, M), jnp.int32),
                   jax.ShapeDtypeStruct((H, PEER_TOPK, M), F32)],
        compiler_params=_params("parallel", "arbitrary"),
        name="peer_route",
    )(q, keys1.astype(BF16), keys2.astype(BF16))


GATHER_GROUP = 4
GATHER_RING = 4
GATHER_SLOTS = GATHER_GROUP * GATHER_RING


def pack_expert_tables(u, v):
    E, D = u.shape
    ub = lax.bitcast_convert_type(u.astype(jnp.bfloat16), jnp.uint16).astype(jnp.uint32)
    vb = lax.bitcast_convert_type(v.astype(jnp.bfloat16), jnp.uint16).astype(jnp.uint32)
    return (ub | (vb << 16)).reshape(E, D // LANES, LANES)


def _gather_body(idx_hbm, wg_ref, x_ref, res_ref, g2_ref, uv_hbm, o_ref, idx_smem, *scratch,
                 n_sel, d):
    slots, (isem, sems) = scratch[:GATHER_SLOTS], scratch[GATHER_SLOTS:]
    sub = d // LANES
    tb = x_ref.shape[0]
    n_idx = tb * n_sel
    step = pl.program_id(0)
    last_step = pl.num_programs(0) - 1
    cur = (step % 2) * n_idx
    nxt = n_idx - cur

    def idx_copy(blk, off):
        return pltpu.make_async_copy(idx_hbm.at[blk], idx_smem.at[pl.ds(off, n_idx)], isem)

    def row_copy(e, j, s):
        return pltpu.make_async_copy(uv_hbm.at[e], slots[s].at[:, j, :], sems.at[s])

    def issue_group(g, ring):
        for k in range(GATHER_GROUP):
            t = g * GATHER_GROUP + k
            beyond = jnp.where(step == last_step, cur + (tb - 1) * n_sel, nxt + (t - tb) * n_sel)
            base = jnp.where(t < tb, cur + t * n_sel, beyond)
            for j in range(n_sel):
                row_copy(idx_smem[base + j], j, ring * GATHER_GROUP + k).start(priority=j % 2)

    def wait_group(ring):
        for k in range(GATHER_GROUP):
            for j in range(n_sel):
                row_copy(0, j, ring * GATHER_GROUP + k).wait()

    ones = jnp.ones((8, LANES), BF16)

    def compute(t, s):
        x = x_ref[pl.ds(t, 1), :]
        p, vs = None, []
        for q in range(sub):
            word = slots[s][q]
            u_q = lax.bitcast_convert_type(word << 16, F32)
            vs.append(lax.bitcast_convert_type(word & jnp.uint32(0xFFFF0000), F32))
            term = u_q * x[:, q * LANES:(q + 1) * LANES]
            p = term if p is None else p + term
        hi = p.astype(BF16)
        lo = (p - hi.astype(F32)).astype(BF16)
        act = (lax.dot_general(ones, hi, NT_DIMS, preferred_element_type=F32)
               + lax.dot_general(ones, lo, NT_DIMS, preferred_element_type=F32))
        gelu = 0.5 * act * (1.0 + lax.erf(act * (2.0 ** -0.5)))
        w = (wg_ref[pl.ds(t, 1), :] * gelu).astype(BF16)
        v = jnp.concatenate(vs, axis=1).astype(BF16)
        out = jnp.dot(w, v, preferred_element_type=F32)
        o_ref[pl.ds(t, 1), :] = res_ref[pl.ds(t, 1), :] + g2_ref[...] * out[0:1]

    @pl.when(step == 0)
    def _():
        first = idx_copy(0, 0)
        first.start()
        first.wait()
        for r in range(GATHER_RING - 1):
            issue_group(r, r)

    @pl.when(step < last_step)
    def _():
        idx_copy(step + 1, nxt).start()

    n_sweeps = tb // GATHER_SLOTS

    def sweep(it, carry):
        @pl.when(jnp.logical_and(it == n_sweeps - 1, step < last_step))
        def _():
            idx_copy(step + 1, nxt).wait()

        for r in range(GATHER_RING):
            g = it * GATHER_RING + r
            wait_group(r)
            issue_group(g + GATHER_RING - 1, (r - 1) % GATHER_RING)
            for k in range(GATHER_GROUP):
                compute(g * GATHER_GROUP + k, r * GATHER_GROUP + k)
        return carry

    lax.fori_loop(0, n_sweeps, sweep, 0)

    @pl.when(step == last_step)
    def _():
        for r in range(GATHER_RING - 1):
            wait_group(r)


def peer_experts(idx, gates, x, res, gate2, uv):
    B, L, D = x.shape
    M = B * L
    n_sel = idx.shape[1]
    tb = _tile(L, 512)
    lb = L // tb
    assert tb % GATHER_SLOTS == 0
    body = functools.partial(_gather_body, n_sel=n_sel, d=D)
    tok = lambda shape_last: pl.BlockSpec((tb, shape_last), lambda i: (i, 0))
    out = pl.pallas_call(
        body,
        grid=(M // tb,),
        in_specs=[pl.BlockSpec(memory_space=pl.ANY), tok(n_sel), tok(D), tok(D),
                  pl.BlockSpec((None, 1, D), lambda i: (i // lb, 0, 0)),
                  pl.BlockSpec(memory_space=pl.ANY)],
        out_specs=tok(D),
        out_shape=jax.ShapeDtypeStruct((M, D), F32),
        scratch_shapes=([pltpu.SMEM((2 * tb * n_sel,), jnp.int32)]
                        + [pltpu.VMEM((D // LANES, n_sel, LANES), jnp.uint32)] * GATHER_SLOTS
                        + [pltpu.SemaphoreType.DMA(()), pltpu.SemaphoreType.DMA((GATHER_SLOTS,))]),
        compiler_params=_params("arbitrary"),
        name="peer_experts",
    )(idx.reshape(M // tb, tb * n_sel), gates, x.reshape(M, D), res.reshape(M, D), gate2, uv)
    return out.reshape(B, L, D)


def peer_mix(h, g, shift, scale, gate2, w_q, keys1, keys2, uv):
    B, L, D = h.shape
    a, a32 = modulate(h, g, shift, scale, out_dtypes=(BF16, F32))
    q = matmul(a, w_q, out_dtype=BF16)
    idx, gates = peer_route(q.reshape(B * L, -1), keys1, keys2)
    n_sel = idx.shape[0] * idx.shape[1]
    idx = idx.reshape(n_sel, B * L).T
    gates = gates.reshape(n_sel, B * L).T
    return peer_experts(idx, gates, a32, h, gate2, uv)


def kernel(x, c, ctx, c_ctx, ada_w, ada_b, norm_mix_g, norm_ffn_g, final_g, attn_w_qkv, attn_w_o, attn_sink, hy_w_in, hy_conv_w, hy_conv_b, hy_f_w1, hy_f_b1, hy_f_w2, hy_f_b2, hy_f_w3, hy_decay, hy_fbias, hy_w_out, peer_w_q, peer_keys1, peer_keys2, peer_u, peer_v):
    B, L, D = x.shape
    C = ctx.shape[1]
    depth = ada_w.shape[0]
    n_mixers = 2
    last_attn = max(i for i in range(depth) if i % n_mixers == 0)
    nq = attn_w_o.shape[1] // HEAD_DIM
    nkv = (attn_w_qkv.shape[2] - nq * HEAD_DIM) // (2 * HEAD_DIM)

    cond = jnp.concatenate([jax.nn.silu(c), jax.nn.silu(c_ctx)[None]], axis=0)
    cond = jnp.pad(cond, ((0, -(B + 1) % 16), (0, 0))).astype(BF16)[None]
    rope = rope_tables(L)
    h_lat, h_ctx = x, ctx

    for i in range(depth):
        is_attn = i % n_mixers == 0
        j = i // n_mixers
        ctx_update = i < last_attn
        mods = matmul(cond, ada_w[i])[0, :B + 1] + ada_b[i]
        lat = [m[:, None, :] for m in jnp.split(mods[:B], N_MODS, axis=-1)]
        cx = [jnp.broadcast_to(m[None], (B, 1, D)) for m in jnp.split(mods[B:], N_MODS, axis=-1)]
        sh1, sc1, g1, sh2, sc2, g2 = lat
        csh1, csc1, cg1, csh2, csc2, cg2 = cx
        a_lat = modulate(h_lat, norm_mix_g[i], sh1, sc1)

        if is_attn:
            w_qkv, w_o = attn_w_qkv[j], attn_w_o[j]
            a_ctx = modulate(h_ctx, norm_mix_g[i], csh1, csc1)
            qkv = matmul(a_lat, w_qkv, mode="rope", out_dtype=BF16, rope=rope,
                         n_rope_cols=(nq + nkv) * HEAD_DIM)
            qkv_ctx = matmul(a_ctx, w_qkv, out_dtype=BF16)
            o_lat = attention(qkv, qkv_ctx, attn_sink[j], nq=nq, nkv=nkv, windowed=True)
            h_lat = matmul(o_lat, w_o, mode="res", res=h_lat, gate=g1)
            if ctx_update:
                o_ctx = attention(qkv_ctx, qkv_ctx, attn_sink[j], nq=nq, nkv=nkv, windowed=False)
                h_ctx = matmul(o_ctx, w_o, mode="res", res=h_ctx, gate=cg1)
        else:
            w_in, w_out = hy_w_in[j], hy_w_out[j]
            f_args = (hy_f_w1[j], hy_f_b1[j], hy_f_w2[j], hy_f_b2[j], hy_f_w3[j], hy_decay[j])
            seqs = [(a_lat, h_lat, g1)]
            if ctx_update:
                seqs.append((modulate(h_ctx, norm_mix_g[i], csh1, csc1), h_ctx, cg1))
            outs = []
            for a_seq, h_seq, gate in seqs:
                Ls = a_seq.shape[1]
                fwd, inv, _ = _dft_matrices(Ls)
                fwd, inv = jnp.asarray(fwd, BF16), jnp.asarray(inv, BF16)
                spec = hyena_spectra(Ls, fwd, hyena_filters(Ls, *f_args))
                outs.append(hyena_mix(a_seq, h_seq, gate, w_in, hy_conv_w[j], hy_conv_b[j],
                                      w_out, hy_fbias[j], (fwd, inv), spec))
            h_lat = outs[0]
            if ctx_update:
                h_ctx = outs[1]

        w_q = peer_w_q[i]
        uv = pack_expert_tables(peer_u[i], peer_v[i])
        h_lat = peer_mix(h_lat, norm_ffn_g[i], sh2, sc2, g2, w_q, peer_keys1[i], peer_keys2[i], uv)
        if ctx_update:
            h_ctx = peer_mix(h_ctx, norm_ffn_g[i], csh2, csc2, cg2, w_q, peer_keys1[i],
                             peer_keys2[i], uv)

    zero = jnp.zeros((B, 1, D), F32)
    return modulate(h_lat, final_g, zero, zero, out_dtypes=(F32,))
```

```python
import functools
import math

import numpy as np
import jax
import jax.numpy as jnp
from jax import lax
from jax.experimental import pallas as pl
from jax.experimental.pallas import tpu as pltpu

F32 = jnp.float32
BF16 = jnp.bfloat16

LANES = 128
HEAD_DIM = 128
GRID_W = 64
ATTN_BLOCK = 128
ROPE_BASE = 10000.0
RMS_EPS = 1e-6
NEG_INF = -1e30
N_MODS = 6
HY_EMB_BANDS = 16
HY_MOD_SHIFT = 0.05
PEER_TOPK = 16
VMEM_LIMIT = 56 * 2 ** 20
NT_DIMS = (((1,), (1,)), ((), ()))


def _params(*sem):
    return pltpu.CompilerParams(dimension_semantics=sem, vmem_limit_bytes=VMEM_LIMIT)


def _tile(n, pref):
    if n <= pref:
        return n
    t = (pref // LANES) * LANES
    while n % t:
        t -= LANES
    return t


def _modulate_body(h_ref, g_ref, sh_ref, sc_ref, *o_refs):
    x = h_ref[...]
    ms = jnp.mean(x * x, axis=-1, keepdims=True)
    y = x * lax.rsqrt(ms + RMS_EPS) * g_ref[...]
    y = y * (1.0 + sc_ref[...]) + sh_ref[...]
    for o_ref in o_refs:
        o_ref[...] = y.astype(o_ref.dtype)


def modulate(h, g, shift, scale, out_dtypes=None):
    B, L, D = h.shape
    out_dtypes = out_dtypes or (BF16,)
    tm = _tile(L, 512)
    vec = pl.BlockSpec((None, 1, D), lambda b, i: (b, 0, 0))
    blk = pl.BlockSpec((None, tm, D), lambda b, i: (b, i, 0))
    outs = pl.pallas_call(
        _modulate_body,
        grid=(B, L // tm),
        in_specs=[blk, pl.BlockSpec((1, D), lambda b, i: (0, 0)), vec, vec],
        out_specs=[blk] * len(out_dtypes),
        out_shape=[jax.ShapeDtypeStruct((B, L, D), dt) for dt in out_dtypes],
        compiler_params=_params("parallel", "parallel"),
        name="modulate",
    )(h, g.reshape(1, D), shift, scale)
    return outs[0] if len(outs) == 1 else outs


def _mm_plain_body(a_ref, w_ref, o_ref):
    acc = jnp.dot(a_ref[...], w_ref[...].astype(BF16), preferred_element_type=F32)
    o_ref[...] = acc.astype(o_ref.dtype)


def _mm_res_body(a_ref, w_ref, r_ref, g_ref, o_ref):
    acc = jnp.dot(a_ref[...], w_ref[...].astype(BF16), preferred_element_type=F32)
    o_ref[...] = r_ref[...] + g_ref[...] * acc


def _mm_rope_body(a_ref, w_ref, cos_ref, sa_ref, sb_ref, o_ref, *, n_rope):
    acc = jnp.dot(a_ref[...], w_ref[...].astype(BF16), preferred_element_type=F32)
    j = pl.program_id(2)

    @pl.when(j < n_rope)
    def _():
        cos, sa, sb = cos_ref[...], sa_ref[...], sb_ref[...]
        quarter = HEAD_DIM // 4
        for s in range(acc.shape[1] // HEAD_DIM):
            x = acc[:, s * HEAD_DIM:(s + 1) * HEAD_DIM]
            y = (x * cos + pltpu.roll(x, quarter, axis=1) * sa
                 + pltpu.roll(x, HEAD_DIM - quarter, axis=1) * sb)
            o_ref[:, s * HEAD_DIM:(s + 1) * HEAD_DIM] = y.astype(o_ref.dtype)

    @pl.when(j >= n_rope)
    def _():
        o_ref[...] = acc.astype(o_ref.dtype)


def _mm_conv3_body(a_ref, w_ref, cw_ref, cb_ref, o_ref):
    y = jnp.dot(a_ref[...], w_ref[...].astype(BF16), preferred_element_type=F32)
    L = y.shape[0]
    row = lax.broadcasted_iota(jnp.int32, y.shape, 0)
    prev = jnp.where(row == 0, 0.0, pltpu.roll(y, 1, axis=0))
    nxt = jnp.where(row == L - 1, 0.0, pltpu.roll(y, L - 1, axis=0))
    o_ref[...] = cb_ref[...] + prev * cw_ref[0:1, :] + y * cw_ref[1:2, :] + nxt * cw_ref[2:3, :]


def matmul(a, w, *, mode="plain", out_dtype=F32, tm=2048, tn=512, res=None, gate=None,
           rope=None, n_rope_cols=0, conv_w=None, conv_b=None):
    B, L, K = a.shape
    N = w.shape[1]
    tm = L if mode == "conv3" else _tile(L, tm)
    if mode == "rope":
        tn = _tile(math.gcd(N, n_rope_cols), tn)
    else:
        tn = _tile(N, 256 if mode == "conv3" else tn)
    grid = (B, L // tm, N // tn)
    a_spec = pl.BlockSpec((None, tm, K), lambda b, i, j: (b, i, 0))
    w_spec = pl.BlockSpec((K, tn), lambda b, i, j: (0, j))
    o_spec = pl.BlockSpec((None, tm, tn), lambda b, i, j: (b, i, j))
    in_specs, args = [a_spec, w_spec], [a, w]
    if mode == "plain":
        body = _mm_plain_body
    elif mode == "res":
        body, out_dtype = _mm_res_body, F32
        in_specs += [o_spec, pl.BlockSpec((None, 1, tn), lambda b, i, j: (b, 0, j))]
        args += [res, gate]
    elif mode == "rope":
        assert n_rope_cols % tn == 0
        body = functools.partial(_mm_rope_body, n_rope=n_rope_cols // tn)
        tab = pl.BlockSpec((tm, HEAD_DIM), lambda b, i, j: (i, 0))
        in_specs += [tab, tab, tab]
        args += list(rope)
    elif mode == "conv3":
        body, out_dtype = _mm_conv3_body, F32
        in_specs += [pl.BlockSpec((3, tn), lambda b, i, j: (0, j)),
                     pl.BlockSpec((1, tn), lambda b, i, j: (0, j))]
        args += [conv_w, conv_b.reshape(1, N)]
    else:
        raise ValueError(mode)
    return pl.pallas_call(
        body, grid=grid, in_specs=in_specs, out_specs=o_spec,
        out_shape=jax.ShapeDtypeStruct((B, L, N), out_dtype),
        compiler_params=_params("parallel", "parallel", "arbitrary"),
        name="mm_" + mode,
    )(*args)


def _attn_body(sink_ref, q_ref, *refs, nkv, grp, nb, windowed):
    if windowed:
        kp_ref, kc_ref, kn_ref, kx_ref, o_ref = refs
    else:
        kx_ref, o_ref = refs
    n = pl.program_id(1)
    scale = HEAD_DIM ** -0.5
    rows = grp * ATTN_BLOCK
    a = lax.broadcasted_iota(jnp.int32, (rows, ATTN_BLOCK), 0) % ATTN_BLOCK
    kb = lax.broadcasted_iota(jnp.int32, (rows, ATTN_BLOCK), 1)

    def head_cols(ref, c):
        return ref[:, c * HEAD_DIM:(c + 1) * HEAD_DIM]

    for h in range(nkv):
        qh = jnp.concatenate([head_cols(q_ref, h * grp + g) for g in range(grp)], axis=0)
        sink = jnp.concatenate(
            [jnp.full((ATTN_BLOCK, 1), sink_ref[h * grp + g], F32) for g in range(grp)], axis=0)

        def scores(ref):
            return lax.dot_general(qh, head_cols(ref, h), NT_DIMS, preferred_element_type=F32) * scale

        pieces = []
        if windowed:
            s_p = jnp.where(jnp.logical_and(kb >= a, n > 0), scores(kp_ref), NEG_INF)
            s_n = jnp.where(jnp.logical_and(kb <= a, n < nb - 1), scores(kn_ref), NEG_INF)
            pieces += [(s_p, kp_ref), (scores(kc_ref), kc_ref), (s_n, kn_ref)]
        pieces.append((scores(kx_ref), kx_ref))

        def lane_blocks(s):
            return [s[:, c:c + LANES] for c in range(0, s.shape[1], LANES)]

        mx = functools.reduce(jnp.maximum, [blk for s, _ in pieces for blk in lane_blocks(s)])
        m = jnp.maximum(sink, jnp.max(mx, axis=-1, keepdims=True))
        psum = jnp.zeros((rows, LANES), F32)
        o = jnp.zeros((rows, HEAD_DIM), F32)
        for s, ref in pieces:
            p = jnp.exp(s - m)
            psum = functools.reduce(jnp.add, lane_blocks(p), psum)
            o = o + jnp.dot(p.astype(BF16), head_cols(ref, nkv + h), preferred_element_type=F32)
        o = o / (jnp.exp(sink - m) + jnp.sum(psum, axis=-1, keepdims=True))
        for g in range(grp):
            c = h * grp + g
            o_ref[:, c * HEAD_DIM:(c + 1) * HEAD_DIM] = (
                o[g * ATTN_BLOCK:(g + 1) * ATTN_BLOCK].astype(o_ref.dtype))


def attention(qkv, qkv_ctx, sink, *, nq, nkv, windowed):
    B, L, _ = qkv.shape
    C = qkv_ctx.shape[1]
    nb = L // ATTN_BLOCK
    qd, kvd = nq * HEAD_DIM, nkv * HEAD_DIM
    assert qd % (2 * kvd) == 0 and C % LANES == 0
    kv_col = qd // (2 * kvd)
    q_spec = pl.BlockSpec((None, ATTN_BLOCK, qd), lambda b, n, s: (b, n, 0))

    def kv_spec(off):
        return pl.BlockSpec(
            (None, ATTN_BLOCK, 2 * kvd),
            lambda b, n, s: (b, jnp.clip(n + off, 0, nb - 1), kv_col))

    ctx_spec = pl.BlockSpec((None, C, 2 * kvd), lambda b, n, s: (b, 0, kv_col))
    if windowed:
        in_specs = [q_spec, kv_spec(-1), kv_spec(0), kv_spec(1), ctx_spec]
        args = [qkv, qkv, qkv, qkv, qkv_ctx]
    else:
        in_specs, args = [q_spec, ctx_spec], [qkv, qkv_ctx]
    body = functools.partial(_attn_body, nkv=nkv, grp=nq // nkv, nb=nb, windowed=windowed)
    return pl.pallas_call(
        body,
        grid_spec=pltpu.PrefetchScalarGridSpec(
            num_scalar_prefetch=1, grid=(B, nb), in_specs=in_specs,
            out_specs=pl.BlockSpec((None, ATTN_BLOCK, qd), lambda b, n, s: (b, n, 0))),
        out_shape=jax.ShapeDtypeStruct((B, L, qd), BF16),
        compiler_params=_params("parallel", "parallel"),
        name="attn_win" if windowed else "attn_ctx",
    )(sink.astype(F32), *args)


def rope_tables(L):
    rows = L // GRID_W
    row = jnp.repeat(jnp.arange(rows), GRID_W).astype(F32)
    col = jnp.tile(jnp.arange(GRID_W), rows).astype(F32)
    n_freq = HEAD_DIM // 4
    inv = ROPE_BASE ** (-jnp.arange(n_freq, dtype=F32) / n_freq)
    ar, ac = row[:, None] * inv, col[:, None] * inv
    ang = jnp.concatenate([ar, ar, ac, ac], axis=-1)
    cos, sin = jnp.cos(ang), jnp.sin(ang)
    odd = (jnp.arange(HEAD_DIM) // n_freq) % 2 == 1
    return cos, jnp.where(odd, sin, 0.0), jnp.where(odd, 0.0, -sin)


@functools.lru_cache(maxsize=None)
def _dft_matrices(L):
    N = 2 * L
    fb = min(256, L)
    nfb = L // fb
    k = np.arange(L, dtype=np.int64)[:, None]
    t = np.arange(L, dtype=np.int64)[None, :]
    ang = 2.0 * np.pi * ((k * t) % N).astype(np.float64) / N
    c, s = np.cos(ang), np.sin(ang)
    nyq = np.where(np.arange(L) % 2 == 0, 1.0, -1.0)
    f_re, f_im = c.copy(), -s
    f_im[0] = nyq
    i_re, i_im = 2.0 * c / N, -2.0 * s / N
    i_re[0] = 1.0 / N
    i_im[0] = nyq / N
    fwd = np.concatenate([f_re.reshape(nfb, fb, L), f_im.reshape(nfb, fb, L)], axis=1)
    inv = np.concatenate([i_re.reshape(nfb, fb, L), i_im.reshape(nfb, fb, L)], axis=1)
    inv = np.transpose(inv, (0, 2, 1))
    return fwd.astype(np.float32), inv.astype(np.float32), fb


def _filter_body(z_ref, w1_ref, b1_ref, w2_ref, b2_ref, w3_ref, dec_ref, o_ref, *, n_inner):
    hi = lax.Precision.HIGHEST
    h = jnp.sin(jnp.dot(z_ref[...], w1_ref[...], precision=hi, preferred_element_type=F32)
                + b1_ref[...])
    for i in range(n_inner):
        h = jnp.sin(jnp.dot(h, w2_ref[i], precision=hi, preferred_element_type=F32)
                    + b2_ref[i])
    y = jnp.dot(h, w3_ref[...], precision=hi, preferred_element_type=F32)
    L = y.shape[0]
    t01 = lax.broadcasted_iota(jnp.int32, y.shape, 0).astype(F32) / L
    o_ref[...] = y * (jnp.exp(-t01 * jnp.abs(dec_ref[...])) + HY_MOD_SHIFT)


def hyena_filters(L, f_w1, f_b1, f_w2, f_b2, f_w3, decay):
    order, D = decay.shape
    hid = f_w1.shape[1]
    n_inner = f_w2.shape[0]
    n_dir = f_w3.shape[1] // (order * D)
    t = np.arange(L, dtype=np.float32)
    bands = np.arange(1, HY_EMB_BANDS + 1, dtype=np.float32)
    ang = (np.float32(2.0 * math.pi) * t[:, None] * bands[None, :] / np.float32(L)).astype(np.float32)
    z = np.concatenate([(t / np.float32(L))[:, None], np.cos(ang), np.sin(ang)], axis=-1)
    emb = z.shape[1]
    emb_pad = -(-emb // 8) * 8
    z = np.pad(z, ((0, 0), (0, emb_pad - emb))).astype(np.float32)
    w1 = jnp.pad(f_w1, ((0, emb_pad - emb), (0, 0)))
    tn = _tile(D, 1024)
    per_o = n_dir * D // tn
    full = lambda shape: pl.BlockSpec(shape, lambda j: (0,) * len(shape))
    out = pl.pallas_call(
        functools.partial(_filter_body, n_inner=n_inner),
        grid=(order * n_dir * D // tn,),
        in_specs=[full((L, emb_pad)), full((emb_pad, hid)), full((1, hid)),
                  full((n_inner, hid, hid)), full((n_inner, 1, hid)),
                  pl.BlockSpec((hid, tn), lambda j: (0, j)),
                  pl.BlockSpec((None, 1, tn), lambda j: (j // per_o, 0, j % (D // tn)))],
        out_specs=pl.BlockSpec((L, tn), lambda j: (0, j)),
        out_shape=jax.ShapeDtypeStruct((L, order * n_dir * D), F32),
        compiler_params=_params("parallel"),
        name="hyena_filter",
    )(jnp.asarray(z), w1, f_b1.reshape(1, hid), f_w2, f_b2.reshape(n_inner, 1, hid), f_w3,
      decay.reshape(order, 1, D))
    return out.reshape(L, order, n_dir, D)


def hyena_spectra(L, fwd, filt):
    order, D = filt.shape[1], filt.shape[3]
    nfb, fb2, _ = fwd.shape
    f = filt[:, :, 0]
    g = filt[:, :, 1].at[0].set(0.0)
    cols = jnp.stack([f + g, f - g], axis=2).reshape(L, order * 2 * D)
    hi = cols.astype(BF16)
    lo = (cols - hi.astype(F32)).astype(BF16)
    fwd2 = jnp.concatenate([fwd, fwd], axis=-1).reshape(1, nfb * fb2, 2 * L)
    spec = matmul(fwd2, jnp.concatenate([hi, lo], axis=0), tm=1024)
    return spec.reshape(nfb, fb2, order * 2 * D)


def _lconv_body(v_ref, x_ref, fw_ref, iv_ref, ha_ref, hb_ref, nyq_ref, bias_ref, o_ref,
                acc_ref, vb_ref, *, fb):
    f = pl.program_id(2)

    @pl.when(f == 0)
    def _():
        vb_ref[...] = v_ref[...].astype(BF16)
        acc_ref[...] = jnp.zeros_like(acc_ref)

    spec = jnp.dot(fw_ref[...], vb_ref[...], preferred_element_type=F32)
    xr, xi = spec[:fb], spec[fb:]
    a = ha_ref[...]
    dc = jnp.logical_and(lax.broadcasted_iota(jnp.int32, a.shape, 0) == 0, f == 0)
    b = jnp.where(dc, 0.0, hb_ref[...])
    c = jnp.where(dc, nyq_ref[0:1, :], a)
    yr = (xr * a - xi * b).astype(BF16)
    yi = (xr * b + xi * c).astype(BF16)
    acc_ref[...] += (jnp.dot(iv_ref[:, :fb], yr, preferred_element_type=F32)
                     + jnp.dot(iv_ref[:, fb:], yi, preferred_element_type=F32))

    @pl.when(f == pl.num_programs(2) - 1)
    def _():
        o_ref[...] = (x_ref[...] * (acc_ref[...] + v_ref[...] * bias_ref[...])).astype(o_ref.dtype)


def long_conv_gated(src, src_blk, gate, gate_blk, fwd, inv, spec, o, bias, out_dtype):
    B, L, _ = src.shape
    D = bias.shape[0]
    nfb, fb2, _ = fwd.shape
    fb = fb2 // 2
    tc = _tile(D, 512)
    ncb = D // tc
    col = lambda blk: pl.BlockSpec((None, L, tc), lambda c, b, f: (b, 0, blk * ncb + c))

    def hspec(part, grp, fixed_f=None):
        return pl.BlockSpec((None, fb, tc), lambda c, b, f: (f if fixed_f is None else fixed_f,
                                                            part, grp * ncb + c))

    return pl.pallas_call(
        functools.partial(_lconv_body, fb=fb),
        grid=(ncb, B, nfb),
        in_specs=[col(src_blk), col(gate_blk),
                  pl.BlockSpec((None, fb2, L), lambda c, b, f: (f, 0, 0)),
                  pl.BlockSpec((None, L, fb2), lambda c, b, f: (f, 0, 0)),
                  hspec(0, 2 * o), hspec(1, 2 * o + 1), hspec(1, 2 * o, fixed_f=0),
                  pl.BlockSpec((1, tc), lambda c, b, f: (0, c))],
        out_specs=pl.BlockSpec((None, L, tc), lambda c, b, f: (b, 0, c)),
        out_shape=jax.ShapeDtypeStruct((B, L, D), out_dtype),
        scratch_shapes=[pltpu.VMEM((L, tc), F32), pltpu.VMEM((L, tc), BF16)],
        compiler_params=_params("parallel", "parallel", "arbitrary"),
        name="long_conv",
    )(src, gate, fwd, inv, spec, spec, spec, bias.reshape(1, D))


def hyena_mix(a, res, gate, w_in, conv_w, conv_b, w_out, fbias, dft, spec):
    fwd, inv = dft
    u = matmul(a, w_in, mode="conv3", conv_w=conv_w, conv_b=conv_b)
    z = long_conv_gated(u, 0, u, 1, fwd, inv, spec, 0, fbias[0], F32)
    y = long_conv_gated(z, 0, u, 2, fwd, inv, spec, 1, fbias[1], BF16)
    return matmul(y, w_out, mode="res", res=res, gate=gate)


def _topk_rows(s, k, key=None):
    if key is None:
        key = lax.broadcasted_iota(jnp.int32, s.shape, 0)
    big = jnp.iinfo(jnp.int32).max
    vals, keys = [], []
    for _ in range(k):
        m = jnp.max(s, axis=0, keepdims=True)
        i = jnp.min(jnp.where(s == m, key, big), axis=0, keepdims=True)
        vals.append(m)
        keys.append(i)
        s = jnp.where(key == i, -jnp.inf, s)
    return vals, keys


def _route_body(q_ref, k1_ref, k2_ref, idx_ref, g_ref, *, half, n_keys):
    T = q_ref.shape[0]
    K = PEER_TOPK
    assert K == 16
    r8 = lax.broadcasted_iota(jnp.int32, (8, LANES), 0)
    r16 = lax.broadcasted_iota(jnp.int32, (K, LANES), 0)
    for c in range(T // LANES):
        q = q_ref[c * LANES:(c + 1) * LANES, :]
        s1 = lax.dot_general(k1_ref[...], q[:, :half], NT_DIMS, preferred_element_type=F32)
        s2 = lax.dot_general(k2_ref[...], q[:, half:], NT_DIMS, preferred_element_type=F32)
        v1, i1 = _topk_rows(s1, K)
        v2, i2 = _topk_rows(s2, K)
        v1c, i1c = jnp.concatenate(v1, axis=0), jnp.concatenate(i1, axis=0)
        v2c, i2c = jnp.concatenate(v2, axis=0), jnp.concatenate(i2, axis=0)
        cand = [v1[0] + v2c]
        cid = [i1[0] * n_keys + i2c]
        pos = [r16]
        for i in range(1, 8):
            n = K // (i + 1)
            val = v1[i] + v2c[:8]
            cand.append(val if n >= 8 else jnp.where(r8 < n, val, -jnp.inf))
            cid.append(i1[i] * n_keys + i2c[:8])
            pos.append(i * K + r8)
        cand.append(v1c[8:] + v2[0])
        cid.append(i1c[8:] * n_keys + i2[0])
        pos.append((8 + r8) * K)
        cand, cid, pos = (jnp.concatenate(x, axis=0) for x in (cand, cid, pos))
        top, sel = _topk_rows(cand, K, key=pos)
        ids = [jnp.max(jnp.where(pos == p, cid, -1), axis=0, keepdims=True) for p in sel]
        e = [jnp.exp(t - top[0]) for t in top]
        den = e[0]
        for x in e[1:]:
            den = den + x
        idx_ref[:, c * LANES:(c + 1) * LANES] = jnp.concatenate(ids, axis=0)
        g_ref[:, c * LANES:(c + 1) * LANES] = jnp.concatenate(e, axis=0) / den


def peer_route(q, keys1, keys2):
    M = q.shape[0]
    H, n_keys, half = keys1.shape
    T = _tile(M, 256)
    kspec = pl.BlockSpec((None, n_keys, half), lambda i, h: (h, 0, 0))
    ospec = pl.BlockSpec((None, PEER_TOPK, T), lambda i, h: (h, 0, i))
    return pl.pallas_call(
        functools.partial(_route_body, half=half, n_keys=n_keys),
        grid=(M // T, H),
        in_specs=[pl.BlockSpec((T, 2 * half), lambda i, h: (i, h)), kspec, kspec],
        out_specs=[ospec, ospec],
        out_shape=[jax.ShapeDtypeStruct((H, PEER_TOPK, M), jnp.int32),
                   jax.ShapeDtypeStruct((H, PEER_TOPK, M), F32)],
        compiler_params=_params("parallel", "arbitrary"),
        name="peer_route",
    )(q, keys1.astype(BF16), keys2.astype(BF16))


GATHER_GROUP = 4
GATHER_RING = 4
GATHER_SLOTS = GATHER_GROUP * GATHER_RING


def pack_expert_tables(u, v):
    E, D = u.shape
    ub = lax.bitcast_convert_type(u.astype(jnp.bfloat16), jnp.uint16).astype(jnp.uint32)
    vb = lax.bitcast_convert_type(v.astype(jnp.bfloat16), jnp.uint16).astype(jnp.uint32)
    return (ub | (vb << 16)).reshape(E, D // LANES, LANES)


def _gather_body(idx_hbm, wg_ref, x_ref, res_ref, g2_ref, uv_hbm, o_ref, idx_smem, *scratch,
                 n_sel, d):
    slots, (isem, sems) = scratch[:GATHER_SLOTS], scratch[GATHER_SLOTS:]
    sub = d // LANES
    tb = x_ref.shape[0]
    n_idx = tb * n_sel
    step = pl.program_id(0)
    last_step = pl.num_programs(0) - 1
    cur = (step % 2) * n_idx
    nxt = n_idx - cur

    def idx_copy(blk, off):
        return pltpu.make_async_copy(idx_hbm.at[blk], idx_smem.at[pl.ds(off, n_idx)], isem)

    def row_copy(e, j, s):
        return pltpu.make_async_copy(uv_hbm.at[e], slots[s].at[:, j, :], sems.at[s])

    def issue_group(g, ring):
        for k in range(GATHER_GROUP):
            t = g * GATHER_GROUP + k
            beyond = jnp.where(step == last_step, cur + (tb - 1) * n_sel, nxt + (t - tb) * n_sel)
            base = jnp.where(t < tb, cur + t * n_sel, beyond)
            for j in range(n_sel):
                row_copy(idx_smem[base + j], j, ring * GATHER_GROUP + k).start(priority=j % 2)

    def wait_group(ring):
        for k in range(GATHER_GROUP):
            for j in range(n_sel):
                row_copy(0, j, ring * GATHER_GROUP + k).wait()

    ones = jnp.ones((8, LANES), BF16)

    def compute(t, s):
        x = x_ref[pl.ds(t, 1), :]
        p, vs = None, []
        for q in range(sub):
            word = slots[s][q]
            u_q = lax.bitcast_convert_type(word << 16, F32)
            vs.append(lax.bitcast_convert_type(word & jnp.uint32(0xFFFF0000), F32))
            term = u_q * x[:, q * LANES:(q + 1) * LANES]
            p = term if p is None else p + term
        hi = p.astype(BF16)
        lo = (p - hi.astype(F32)).astype(BF16)
        act = (lax.dot_general(ones, hi, NT_DIMS, preferred_element_type=F32)
               + lax.dot_general(ones, lo, NT_DIMS, preferred_element_type=F32))
        gelu = 0.5 * act * (1.0 + lax.erf(act * (2.0 ** -0.5)))
        w = (wg_ref[pl.ds(t, 1), :] * gelu).astype(BF16)
        v = jnp.concatenate(vs, axis=1).astype(BF16)
        out = jnp.dot(w, v, preferred_element_type=F32)
        o_ref[pl.ds(t, 1), :] = res_ref[pl.ds(t, 1), :] + g2_ref[...] * out[0:1]

    @pl.when(step == 0)
    def _():
        first = idx_copy(0, 0)
        first.start()
        first.wait()
        for r in range(GATHER_RING - 1):
            issue_group(r, r)

    @pl.when(step < last_step)
    def _():
        idx_copy(step + 1, nxt).start()

    n_sweeps = tb // GATHER_SLOTS

    def sweep(it, carry):
        @pl.when(jnp.logical_and(it == n_sweeps - 1, step < last_step))
        def _():
            idx_copy(step + 1, nxt).wait()

        for r in range(GATHER_RING):
            g = it * GATHER_RING + r
            wait_group(r)
            issue_group(g + GATHER_RING - 1, (r - 1) % GATHER_RING)
            for k in range(GATHER_GROUP):
                compute(g * GATHER_GROUP + k, r * GATHER_GROUP + k)
        return carry

    lax.fori_loop(0, n_sweeps, sweep, 0)

    @pl.when(step == last_step)
    def _():
        for r in range(GATHER_RING - 1):
            wait_group(r)


def peer_experts(idx, gates, x, res, gate2, uv):
    B, L, D = x.shape
    M = B * L
    n_sel = idx.shape[1]
    tb = _tile(L, 512)
    lb = L // tb
    assert tb % GATHER_SLOTS == 0
    body = functools.partial(_gather_body, n_sel=n_sel, d=D)
    tok = lambda shape_last: pl.BlockSpec((tb, shape_last), lambda i: (i, 0))
    out = pl.pallas_call(
        body,
        grid=(M // tb,),
        in_specs=[pl.BlockSpec(memory_space=pl.ANY), tok(n_sel), tok(D), tok(D),
                  pl.BlockSpec((None, 1, D), lambda i: (i // lb, 0, 0)),
                  pl.BlockSpec(memory_space=pl.ANY)],
        out_specs=tok(D),
        out_shape=jax.ShapeDtypeStruct((M, D), F32),
        scratch_shapes=([pltpu.SMEM((2 * tb * n_sel,), jnp.int32)]
                        + [pltpu.VMEM((D // LANES, n_sel, LANES), jnp.uint32)] * GATHER_SLOTS
                        + [pltpu.SemaphoreType.DMA(()), pltpu.SemaphoreType.DMA((GATHER_SLOTS,))]),
        compiler_params=_params("arbitrary"),
        name="peer_experts",
    )(idx.reshape(M // tb, tb * n_sel), gates, x.reshape(M, D), res.reshape(M, D), gate2, uv)
    return out.reshape(B, L, D)


def peer_mix(h, g, shift, scale, gate2, w_q, keys1, keys2, uv):
    B, L, D = h.shape
    a, a32 = modulate(h, g, shift, scale, out_dtypes=(BF16, F32))
    q = matmul(a, w_q, out_dtype=BF16)
    idx, gates = peer_route(q.reshape(B * L, -1), keys1, keys2)
    n_sel = idx.shape[0] * idx.shape[1]
    idx = idx.reshape(n_sel, B * L).T
    gates = gates.reshape(n_sel, B * L).T
    return peer_experts(idx, gates, a32, h, gate2, uv)


def kernel(x, c, ctx, c_ctx, ada_w, ada_b, norm_mix_g, norm_ffn_g, final_g, attn_w_qkv, attn_w_o, attn_sink, hy_w_in, hy_conv_w, hy_conv_b, hy_f_w1, hy_f_b1, hy_f_w2, hy_f_b2, hy_f_w3, hy_decay, hy_fbias, hy_w_out, peer_w_q, peer_keys1, peer_keys2, peer_u, peer_v):
    B, L, D = x.shape
    C = ctx.shape[1]
    depth = ada_w.shape[0]
    n_mixers = 2
    last_attn = max(i for i in range(depth) if i % n_mixers == 0)
    nq = attn_w_o.shape[1] // HEAD_DIM
    nkv = (attn_w_qkv.shape[2] - nq * HEAD_DIM) // (2 * HEAD_DIM)

    cond = jnp.concatenate([jax.nn.silu(c), jax.nn.silu(c_ctx)[None]], axis=0)
    cond = jnp.pad(cond, ((0, -(B + 1) % 16), (0, 0))).astype(BF16)[None]
    rope = rope_tables(L)
    h_lat, h_ctx = x, ctx

    for i in range(depth):
        is_attn = i % n_mixers == 0
        j = i // n_mixers
        ctx_update = i < last_attn
        mods = matmul(cond, ada_w[i])[0, :B + 1] + ada_b[i]
        lat = [m[:, None, :] for m in jnp.split(mods[:B], N_MODS, axis=-1)]
        cx = [jnp.broadcast_to(m[None], (B, 1, D)) for m in jnp.split(mods[B:], N_MODS, axis=-1)]
        sh1, sc1, g1, sh2, sc2, g2 = lat
        csh1, csc1, cg1, csh2, csc2, cg2 = cx
        a_lat = modulate(h_lat, norm_mix_g[i], sh1, sc1)

        if is_attn:
            w_qkv, w_o = attn_w_qkv[j], attn_w_o[j]
            a_ctx = modulate(h_ctx, norm_mix_g[i], csh1, csc1)
            qkv = matmul(a_lat, w_qkv, mode="rope", out_dtype=BF16, rope=rope,
                         n_rope_cols=(nq + nkv) * HEAD_DIM)
            qkv_ctx = matmul(a_ctx, w_qkv, out_dtype=BF16)
            o_lat = attention(qkv, qkv_ctx, attn_sink[j], nq=nq, nkv=nkv, windowed=True)
            h_lat = matmul(o_lat, w_o, mode="res", res=h_lat, gate=g1)
            if ctx_update:
                o_ctx = attention(qkv_ctx, qkv_ctx, attn_sink[j], nq=nq, nkv=nkv, windowed=False)
                h_ctx = matmul(o_ctx, w_o, mode="res", res=h_ctx, gate=cg1)
        else:
            w_in, w_out = hy_w_in[j], hy_w_out[j]
            f_args = (hy_f_w1[j], hy_f_b1[j], hy_f_w2[j], hy_f_b2[j], hy_f_w3[j], hy_decay[j])
            seqs = [(a_lat, h_lat, g1)]
            if ctx_update:
                seqs.append((modulate(h_ctx, norm_mix_g[i], csh1, csc1), h_ctx, cg1))
            outs = []
            for a_seq, h_seq, gate in seqs:
                Ls = a_seq.shape[1]
                fwd, inv, _ = _dft_matrices(Ls)
                fwd, inv = jnp.asarray(fwd, BF16), jnp.asarray(inv, BF16)
                spec = hyena_spectra(Ls, fwd, hyena_filters(Ls, *f_args))
                outs.append(hyena_mix(a_seq, h_seq, gate, w_in, hy_conv_w[j], hy_conv_b[j],
                                      w_out, hy_fbias[j], (fwd, inv), spec))
            h_lat = outs[0]
            if ctx_update:
                h_ctx = outs[1]

        w_q = peer_w_q[i]
        uv = pack_expert_tables(peer_u[i], peer_v[i])
        h_lat = peer_mix(h_lat, norm_ffn_g[i], sh2, sc2, g2, w_q, peer_keys1[i], peer_keys2[i], uv)
        if ctx_update:
            h_ctx = peer_mix(h_ctx, norm_ffn_g[i], csh2, csc2, cg2, w_q, peer_keys1[i],
                             peer_keys2[i], uv)

    zero = jnp.zeros((B, 1, D), F32)
    return modulate(h_lat, final_g, zero, zero, out_dtypes=(F32,))
```

```python
import functools
import math

import numpy as np
import jax
import jax.numpy as jnp
from jax import lax
from jax.experimental import pallas as pl
from jax.experimental.pallas import tpu as pltpu

F32 = jnp.float32
BF16 = jnp.bfloat16

LANES = 128
HEAD_DIM = 128
GRID_W = 64
ATTN_BLOCK = 128
ROPE_BASE = 10000.0
RMS_EPS = 1e-6
NEG_INF = -1e30
N_MODS = 6
HY_EMB_BANDS = 16
HY_MOD_SHIFT = 0.05
PEER_TOPK = 16
VMEM_LIMIT = 56 * 2 ** 20
NT_DIMS = (((1,), (1,)), ((), ()))


def _params(*sem):
    return pltpu.CompilerParams(dimension_semantics=sem, vmem_limit_bytes=VMEM_LIMIT)


def _tile(n, pref):
    if n <= pref:
        return n
    t = (pref // LANES) * LANES
    while n % t:
        t -= LANES
    return t


def _modulate_body(h_ref, g_ref, sh_ref, sc_ref, *o_refs):
    x = h_ref[...]
    ms = jnp.mean(x * x, axis=-1, keepdims=True)
    y = x * lax.rsqrt(ms + RMS_EPS) * g_ref[...]
    y = y * (1.0 + sc_ref[...]) + sh_ref[...]
    for o_ref in o_refs:
        o_ref[...] = y.astype(o_ref.dtype)


def modulate(h, g, shift, scale, out_dtypes=None):
    B, L, D = h.shape
    out_dtypes = out_dtypes or (BF16,)
    tm = _tile(L, 512)
    vec = pl.BlockSpec((None, 1, D), lambda b, i: (b, 0, 0))
    blk = pl.BlockSpec((None, tm, D), lambda b, i: (b, i, 0))
    outs = pl.pallas_call(
        _modulate_body,
        grid=(B, L // tm),
        in_specs=[blk, pl.BlockSpec((1, D), lambda b, i: (0, 0)), vec, vec],
        out_specs=[blk] * len(out_dtypes),
        out_shape=[jax.ShapeDtypeStruct((B, L, D), dt) for dt in out_dtypes],
        compiler_params=_params("parallel", "parallel"),
        name="modulate",
    )(h, g.reshape(1, D), shift, scale)
    return outs[0] if len(outs) == 1 else outs


def _mm_plain_body(a_ref, w_ref, o_ref):
    acc = jnp.dot(a_ref[...], w_ref[...].astype(BF16), preferred_element_type=F32)
    o_ref[...] = acc.astype(o_ref.dtype)


def _mm_res_body(a_ref, w_ref, r_ref, g_ref, o_ref):
    acc = jnp.dot(a_ref[...], w_ref[...].astype(BF16), preferred_element_type=F32)
    o_ref[...] = r_ref[...] + g_ref[...] * acc


def _mm_rope_body(a_ref, w_ref, cos_ref, sa_ref, sb_ref, o_ref, *, n_rope):
    acc = jnp.dot(a_ref[...], w_ref[...].astype(BF16), preferred_element_type=F32)
    j = pl.program_id(2)

    @pl.when(j < n_rope)
    def _():
        cos, sa, sb = cos_ref[...], sa_ref[...], sb_ref[...]
        quarter = HEAD_DIM // 4
        for s in range(acc.shape[1] // HEAD_DIM):
            x = acc[:, s * HEAD_DIM:(s + 1) * HEAD_DIM]
            y = (x * cos + pltpu.roll(x, quarter, axis=1) * sa
                 + pltpu.roll(x, HEAD_DIM - quarter, axis=1) * sb)
            o_ref[:, s * HEAD_DIM:(s + 1) * HEAD_DIM] = y.astype(o_ref.dtype)

    @pl.when(j >= n_rope)
    def _():
        o_ref[...] = acc.astype(o_ref.dtype)


def _mm_conv3_body(a_ref, w_ref, cw_ref, cb_ref, o_ref):
    y = jnp.dot(a_ref[...], w_ref[...].astype(BF16), preferred_element_type=F32)
    L = y.shape[0]
    row = lax.broadcasted_iota(jnp.int32, y.shape, 0)
    prev = jnp.where(row == 0, 0.0, pltpu.roll(y, 1, axis=0))
    nxt = jnp.where(row == L - 1, 0.0, pltpu.roll(y, L - 1, axis=0))
    o_ref[...] = cb_ref[...] + prev * cw_ref[0:1, :] + y * cw_ref[1:2, :] + nxt * cw_ref[2:3, :]


def matmul(a, w, *, mode="plain", out_dtype=F32, tm=2048, tn=512, res=None, gate=None,
           rope=None, n_rope_cols=0, conv_w=None, conv_b=None):
    B, L, K = a.shape
    N = w.shape[1]
    tm = L if mode == "conv3" else _tile(L, tm)
    if mode == "rope":
        tn = _tile(math.gcd(N, n_rope_cols), tn)
    else:
        tn = _tile(N, 256 if mode == "conv3" else tn)
    grid = (B, L // tm, N // tn)
    a_spec = pl.BlockSpec((None, tm, K), lambda b, i, j: (b, i, 0))
    w_spec = pl.BlockSpec((K, tn), lambda b, i, j: (0, j))
    o_spec = pl.BlockSpec((None, tm, tn), lambda b, i, j: (b, i, j))
    in_specs, args = [a_spec, w_spec], [a, w]
    if mode == "plain":
        body = _mm_plain_body
    elif mode == "res":
        body, out_dtype = _mm_res_body, F32
        in_specs += [o_spec, pl.BlockSpec((None, 1, tn), lambda b, i, j: (b, 0, j))]
        args += [res, gate]
    elif mode == "rope":
        assert n_rope_cols % tn == 0
        body = functools.partial(_mm_rope_body, n_rope=n_rope_cols // tn)
        tab = pl.BlockSpec((tm, HEAD_DIM), lambda b, i, j: (i, 0))
        in_specs += [tab, tab, tab]
        args += list(rope)
    elif mode == "conv3":
        body, out_dtype = _mm_conv3_body, F32
        in_specs += [pl.BlockSpec((3, tn), lambda b, i, j: (0, j)),
                     pl.BlockSpec((1, tn), lambda b, i, j: (0, j))]
        args += [conv_w, conv_b.reshape(1, N)]
    else:
        raise ValueError(mode)
    return pl.pallas_call(
        body, grid=grid, in_specs=in_specs, out_specs=o_spec,
        out_shape=jax.ShapeDtypeStruct((B, L, N), out_dtype),
        compiler_params=_params("parallel", "parallel", "arbitrary"),
        name="mm_" + mode,
    )(*args)


def _attn_body(sink_ref, q_ref, *refs, nkv, grp, nb, windowed):
    if windowed:
        kp_ref, kc_ref, kn_ref, kx_ref, o_ref = refs
    else:
        kx_ref, o_ref = refs
    n = pl.program_id(1)
    scale = HEAD_DIM ** -0.5
    rows = grp * ATTN_BLOCK
    a = lax.broadcasted_iota(jnp.int32, (rows, ATTN_BLOCK), 0) % ATTN_BLOCK
    kb = lax.broadcasted_iota(jnp.int32, (rows, ATTN_BLOCK), 1)

    def head_cols(ref, c):
        return ref[:, c * HEAD_DIM:(c + 1) * HEAD_DIM]

    for h in range(nkv):
        qh = jnp.concatenate([head_cols(q_ref, h * grp + g) for g in range(grp)], axis=0)
        sink = jnp.concatenate(
            [jnp.full((ATTN_BLOCK, 1), sink_ref[h * grp + g], F32) for g in range(grp)], axis=0)

        def scores(ref):
            return lax.dot_general(qh, head_cols(ref, h), NT_DIMS, preferred_element_type=F32) * scale

        pieces = []
        if windowed:
            s_p = jnp.where(jnp.logical_and(kb >= a, n > 0), scores(kp_ref), NEG_INF)
            s_n = jnp.where(jnp.logical_and(kb <= a, n < nb - 1), scores(kn_ref), NEG_INF)
            pieces += [(s_p, kp_ref), (scores(kc_ref), kc_ref), (s_n, kn_ref)]
        pieces.append((scores(kx_ref), kx_ref))

        def lane_blocks(s):
            return [s[:, c:c + LANES] for c in range(0, s.shape[1], LANES)]

        mx = functools.reduce(jnp.maximum, [blk for s, _ in pieces for blk in lane_blocks(s)])
        m = jnp.maximum(sink, jnp.max(mx, axis=-1, keepdims=True))
        psum = jnp.zeros((rows, LANES), F32)
        o = jnp.zeros((rows, HEAD_DIM), F32)
        for s, ref in pieces:
            p = jnp.exp(s - m)
            psum = functools.reduce(jnp.add, lane_blocks(p), psum)
            o = o + jnp.dot(p.astype(BF16), head_cols(ref, nkv + h), preferred_element_type=F32)
        o = o / (jnp.exp(sink - m) + jnp.sum(psum, axis=-1, keepdims=True))
        for g in range(grp):
            c = h * grp + g
            o_ref[:, c * HEAD_DIM:(c + 1) * HEAD_DIM] = (
                o[g * ATTN_BLOCK:(g + 1) * ATTN_BLOCK].astype(o_ref.dtype))


def attention(qkv, qkv_ctx, sink, *, nq, nkv, windowed):
    B, L, _ = qkv.shape
    C = qkv_ctx.shape[1]
    nb = L // ATTN_BLOCK
    qd, kvd = nq * HEAD_DIM, nkv * HEAD_DIM
    assert qd % (2 * kvd) == 0 and C % LANES == 0
    kv_col = qd // (2 * kvd)
    q_spec = pl.BlockSpec((None, ATTN_BLOCK, qd), lambda b, n, s: (b, n, 0))

    def kv_spec(off):
        return pl.BlockSpec(
            (None, ATTN_BLOCK, 2 * kvd),
            lambda b, n, s: (b, jnp.clip(n + off, 0, nb - 1), kv_col))

    ctx_spec = pl.BlockSpec((None, C, 2 * kvd), lambda b, n, s: (b, 0, kv_col))
    if windowed:
        in_specs = [q_spec, kv_spec(-1), kv_spec(0), kv_spec(1), ctx_spec]
        args = [qkv, qkv, qkv, qkv, qkv_ctx]
    else:
        in_specs, args = [q_spec, ctx_spec], [qkv, qkv_ctx]
    body = functools.partial(_attn_body, nkv=nkv, grp=nq // nkv, nb=nb, windowed=windowed)
    return pl.pallas_call(
        body,
        grid_spec=pltpu.PrefetchScalarGridSpec(
            num_scalar_prefetch=1, grid=(B, nb), in_specs=in_specs,
            out_specs=pl.BlockSpec((None, ATTN_BLOCK, qd), lambda b, n, s: (b, n, 0))),
        out_shape=jax.ShapeDtypeStruct((B, L, qd), BF16),
        compiler_params=_params("parallel", "parallel"),
        name="attn_win" if windowed else "attn_ctx",
    )(sink.astype(F32), *args)


def rope_tables(L):
    rows = L // GRID_W
    row = jnp.repeat(jnp.arange(rows), GRID_W).astype(F32)
    col = jnp.tile(jnp.arange(GRID_W), rows).astype(F32)
    n_freq = HEAD_DIM // 4
    inv = ROPE_BASE ** (-jnp.arange(n_freq, dtype=F32) / n_freq)
    ar, ac = row[:, None] * inv, col[:, None] * inv
    ang = jnp.concatenate([ar, ar, ac, ac], axis=-1)
    cos, sin = jnp.cos(ang), jnp.sin(ang)
    odd = (jnp.arange(HEAD_DIM) // n_freq) % 2 == 1
    return cos, jnp.where(odd, sin, 0.0), jnp.where(odd, 0.0, -sin)


@functools.lru_cache(maxsize=None)
def _dft_matrices(L):
    N = 2 * L
    fb = min(256, L)
    nfb = L // fb
    k = np.arange(L, dtype=np.int64)[:, None]
    t = np.arange(L, dtype=np.int64)[None, :]
    ang = 2.0 * np.pi * ((k * t) % N).astype(np.float64) / N
    c, s = np.cos(ang), np.sin(ang)
    nyq = np.where(np.arange(L) % 2 == 0, 1.0, -1.0)
    f_re, f_im = c.copy(), -s
    f_im[0] = nyq
    i_re, i_im = 2.0 * c / N, -2.0 * s / N
    i_re[0] = 1.0 / N
    i_im[0] = nyq / N
    fwd = np.concatenate([f_re.reshape(nfb, fb, L), f_im.reshape(nfb, fb, L)], axis=1)
    inv = np.concatenate([i_re.reshape(nfb, fb, L), i_im.reshape(nfb, fb, L)], axis=1)
    inv = np.transpose(inv, (0, 2, 1))
    return fwd.astype(np.float32), inv.astype(np.float32), fb


def _filter_body(z_ref, w1_ref, b1_ref, w2_ref, b2_ref, w3_ref, dec_ref, o_ref, *, n_inner):
    hi = lax.Precision.HIGHEST
    h = jnp.sin(jnp.dot(z_ref[...], w1_ref[...], precision=hi, preferred_element_type=F32)
                + b1_ref[...])
    for i in range(n_inner):
        h = jnp.sin(jnp.dot(h, w2_ref[i], precision=hi, preferred_element_type=F32)
                    + b2_ref[i])
    y = jnp.dot(h, w3_ref[...], precision=hi, preferred_element_type=F32)
    L = y.shape[0]
    t01 = lax.broadcasted_iota(jnp.int32, y.shape, 0).astype(F32) / L
    o_ref[...] = y * (jnp.exp(-t01 * jnp.abs(dec_ref[...])) + HY_MOD_SHIFT)


def hyena_filters(L, f_w1, f_b1, f_w2, f_b2, f_w3, decay):
    order, D = decay.shape
    hid = f_w1.shape[1]
    n_inner = f_w2.shape[0]
    n_dir = f_w3.shape[1] // (order * D)
    t = np.arange(L, dtype=np.float32)
    bands = np.arange(1, HY_EMB_BANDS + 1, dtype=np.float32)
    ang = (np.float32(2.0 * math.pi) * t[:, None] * bands[None, :] / np.float32(L)).astype(np.float32)
    z = np.concatenate([(t / np.float32(L))[:, None], np.cos(ang), np.sin(ang)], axis=-1)
    emb = z.shape[1]
    emb_pad = -(-emb // 8) * 8
    z = np.pad(z, ((0, 0), (0, emb_pad - emb))).astype(np.float32)
    w1 = jnp.pad(f_w1, ((0, emb_pad - emb), (0, 0)))
    tn = _tile(D, 1024)
    per_o = n_dir * D // tn
    full = lambda shape: pl.BlockSpec(shape, lambda j: (0,) * len(shape))
    out = pl.pallas_call(
        functools.partial(_filter_body, n_inner=n_inner),
        grid=(order * n_dir * D // tn,),
        in_specs=[full((L, emb_pad)), full((emb_pad, hid)), full((1, hid)),
                  full((n_inner, hid, hid)), full((n_inner, 1, hid)),
                  pl.BlockSpec((hid, tn), lambda j: (0, j)),
                  pl.BlockSpec((None, 1, tn), lambda j: (j // per_o, 0, j % (D // tn)))],
        out_specs=pl.BlockSpec((L, tn), lambda j: (0, j)),
        out_shape=jax.ShapeDtypeStruct((L, order * n_dir * D), F32),
        compiler_params=_params("parallel"),
        name="hyena_filter",
    )(jnp.asarray(z), w1, f_b1.reshape(1, hid), f_w2, f_b2.reshape(n_inner, 1, hid), f_w3,
      decay.reshape(order, 1, D))
    return out.reshape(L, order, n_dir, D)


def hyena_spectra(L, fwd, filt):
    order, D = filt.shape[1], filt.shape[3]
    nfb, fb2, _ = fwd.shape
    f = filt[:, :, 0]
    g = filt[:, :, 1].at[0].set(0.0)
    cols = jnp.stack([f + g, f - g], axis=2).reshape(L, order * 2 * D)
    hi = cols.astype(BF16)
    lo = (cols - hi.astype(F32)).astype(BF16)
    fwd2 = jnp.concatenate([fwd, fwd], axis=-1).reshape(1, nfb * fb2, 2 * L)
    spec = matmul(fwd2, jnp.concatenate([hi, lo], axis=0), tm=1024)
    return spec.reshape(nfb, fb2, order * 2 * D)


def _lconv_body(v_ref, x_ref, fw_ref, iv_ref, ha_ref, hb_ref, nyq_ref, bias_ref, o_ref,
                acc_ref, vb_ref, *, fb):
    f = pl.program_id(2)

    @pl.when(f == 0)
    def _():
        vb_ref[...] = v_ref[...].astype(BF16)
        acc_ref[...] = jnp.zeros_like(acc_ref)

    spec = jnp.dot(fw_ref[...], vb_ref[...], preferred_element_type=F32)
    xr, xi = spec[:fb], spec[fb:]
    a = ha_ref[...]
    dc = jnp.logical_and(lax.broadcasted_iota(jnp.int32, a.shape, 0) == 0, f == 0)
    b = jnp.where(dc, 0.0, hb_ref[...])
    c = jnp.where(dc, nyq_ref[0:1, :], a)
    yr = (xr * a - xi * b).astype(BF16)
    yi = (xr * b + xi * c).astype(BF16)
    acc_ref[...] += (jnp.dot(iv_ref[:, :fb], yr, preferred_element_type=F32)
                     + jnp.dot(iv_ref[:, fb:], yi, preferred_element_type=F32))

    @pl.when(f == pl.num_programs(2) - 1)
    def _():
        o_ref[...] = (x_ref[...] * (acc_ref[...] + v_ref[...] * bias_ref[...])).astype(o_ref.dtype)


def long_conv_gated(src, src_blk, gate, gate_blk, fwd, inv, spec, o, bias, out_dtype):
    B, L, _ = src.shape
    D = bias.shape[0]
    nfb, fb2, _ = fwd.shape
    fb = fb2 // 2
    tc = _tile(D, 512)
    ncb = D // tc
    col = lambda blk: pl.BlockSpec((None, L, tc), lambda c, b, f: (b, 0, blk * ncb + c))

    def hspec(part, grp, fixed_f=None):
        return pl.BlockSpec((None, fb, tc), lambda c, b, f: (f if fixed_f is None else fixed_f,
                                                            part, grp * ncb + c))

    return pl.pallas_call(
        functools.partial(_lconv_body, fb=fb),
        grid=(ncb, B, nfb),
        in_specs=[col(src_blk), col(gate_blk),
                  pl.BlockSpec((None, fb2, L), lambda c, b, f: (f, 0, 0)),
                  pl.BlockSpec((None, L, fb2), lambda c, b, f: (f, 0, 0)),
                  hspec(0, 2 * o), hspec(1, 2 * o + 1), hspec(1, 2 * o, fixed_f=0),
                  pl.BlockSpec((1, tc), lambda c, b, f: (0, c))],
        out_specs=pl.BlockSpec((None, L, tc), lambda c, b, f: (b, 0, c)),
        out_shape=jax.ShapeDtypeStruct((B, L, D), out_dtype),
        scratch_shapes=[pltpu.VMEM((L, tc), F32), pltpu.VMEM((L, tc), BF16)],
        compiler_params=_params("parallel", "parallel", "arbitrary"),
        name="long_conv",
    )(src, gate, fwd, inv, spec, spec, spec, bias.reshape(1, D))


def hyena_mix(a, res, gate, w_in, conv_w, conv_b, w_out, fbias, dft, spec):
    fwd, inv = dft
    u = matmul(a, w_in, mode="conv3", conv_w=conv_w, conv_b=conv_b)
    z = long_conv_gated(u, 0, u, 1, fwd, inv, spec, 0, fbias[0], F32)
    y = long_conv_gated(z, 0, u, 2, fwd, inv, spec, 1, fbias[1], BF16)
    return matmul(y, w_out, mode="res", res=res, gate=gate)


def _topk_rows(s, k, key=None):
    if key is None:
        key = lax.broadcasted_iota(jnp.int32, s.shape, 0)
    big = jnp.iinfo(jnp.int32).max
    vals, keys = [], []
    for _ in range(k):
        m = jnp.max(s, axis=0, keepdims=True)
        i = jnp.min(jnp.where(s == m, key, big), axis=0, keepdims=True)
        vals.append(m)
        keys.append(i)
        s = jnp.where(key == i, -jnp.inf, s)
    return vals, keys


def _route_body(q_ref, k1_ref, k2_ref, idx_ref, g_ref, *, half, n_keys):
    T = q_ref.shape[0]
    K = PEER_TOPK
    assert K == 16
    r8 = lax.broadcasted_iota(jnp.int32, (8, LANES), 0)
    r16 = lax.broadcasted_iota(jnp.int32, (K, LANES), 0)
    for c in range(T // LANES):
        q = q_ref[c * LANES:(c + 1) * LANES, :]
        s1 = lax.dot_general(k1_ref[...], q[:, :half], NT_DIMS, preferred_element_type=F32)
        s2 = lax.dot_general(k2_ref[...], q[:, half:], NT_DIMS, preferred_element_type=F32)
        v1, i1 = _topk_rows(s1, K)
        v2, i2 = _topk_rows(s2, K)
        v1c, i1c = jnp.concatenate(v1, axis=0), jnp.concatenate(i1, axis=0)
        v2c, i2c = jnp.concatenate(v2, axis=0), jnp.concatenate(i2, axis=0)
        cand = [v1[0] + v2c]
        cid = [i1[0] * n_keys + i2c]
        pos = [r16]
        for i in range(1, 8):
            n = K // (i + 1)
            val = v1[i] + v2c[:8]
            cand.append(val if n >= 8 else jnp.where(r8 < n, val, -jnp.inf))
            cid.append(i1[i] * n_keys + i2c[:8])
            pos.append(i * K + r8)
        cand.append(v1c[8:] + v2[0])
        cid.append(i1c[8:] * n_keys + i2[0])
        pos.append((8 + r8) * K)
        cand, cid, pos = (jnp.concatenate(x, axis=0) for x in (cand, cid, pos))
        top, sel = _topk_rows(cand, K, key=pos)
        ids = [jnp.max(jnp.where(pos == p, cid, -1), axis=0, keepdims=True) for p in sel]
        e = [jnp.exp(t - top[0]) for t in top]
        den = e[0]
        for x in e[1:]:
            den = den + x
        idx_ref[:, c * LANES:(c + 1) * LANES] = jnp.concatenate(ids, axis=0)
        g_ref[:, c * LANES:(c + 1) * LANES] = jnp.concatenate(e, axis=0) / den


def peer_route(q, keys1, keys2):
    M = q.shape[0]
    H, n_keys, half = keys1.shape
    T = _tile(M, 512)
    kspec = pl.BlockSpec((None, n_keys, half), lambda i, h: (h, 0, 0))
    ospec = pl.BlockSpec((None, PEER_TOPK, T), lambda i, h: (h, 0, i))
    return pl.pallas_call(
        functools.partial(_route_body, half=half, n_keys=n_keys),
        grid=(M // T, H),
        in_specs=[pl.BlockSpec((T, 2 * half), lambda i, h: (i, h)), kspec, kspec],
        out_specs=[ospec, ospec],
        out_shape=[jax.ShapeDtypeStruct((H, PEER_TOPK, M), jnp.int32),
                   jax.ShapeDtypeStruct((H, PEER_TOPK, M), F32)],
        compiler_params=_params("parallel", "arbitrary"),
        name="peer_route",
    )(q, keys1.astype(BF16), keys2.astype(BF16))


GATHER_GROUP = 4
GATHER_RING = 4
GATHER_SLOTS = GATHER_GROUP * GATHER_RING


def pack_expert_tables(u, v):
    E, D = u.shape
    ub = lax.bitcast_convert_type(u.astype(jnp.bfloat16), jnp.uint16).astype(jnp.uint32)
    vb = lax.bitcast_convert_type(v.astype(jnp.bfloat16), jnp.uint16).astype(jnp.uint32)
    return (ub | (vb << 16)).reshape(E, D // LANES, LANES)


def _gather_body(idx_hbm, wg_ref, res_ref, ng_ref, sh_ref, sc_ref, g2_ref, uv_hbm, o_ref, idx_smem,
                 *scratch, n_sel, d):
    slots, (isem, sems) = scratch[:GATHER_SLOTS], scratch[GATHER_SLOTS:]
    sub = d // LANES
    tb = res_ref.shape[0]
    n_idx = tb * n_sel
    step = pl.program_id(0)
    last_step = pl.num_programs(0) - 1
    cur = (step % 2) * n_idx
    nxt = n_idx - cur

    def idx_copy(blk, off):
        return pltpu.make_async_copy(idx_hbm.at[blk], idx_smem.at[pl.ds(off, n_idx)], isem)

    def row_copy(e, j, s):
        return pltpu.make_async_copy(uv_hbm.at[e], slots[s].at[:, j, :], sems.at[s])

    def issue_group(g, ring):
        for k in range(GATHER_GROUP):
            t = g * GATHER_GROUP + k
            beyond = jnp.where(step == last_step, cur + (tb - 1) * n_sel, nxt + (t - tb) * n_sel)
            base = jnp.where(t < tb, cur + t * n_sel, beyond)
            for j in range(n_sel):
                row_copy(idx_smem[base + j], j, ring * GATHER_GROUP + k).start(priority=j % 2)

    def wait_group(ring):
        for k in range(GATHER_GROUP):
            for j in range(n_sel):
                row_copy(0, j, ring * GATHER_GROUP + k).wait()

    ones = jnp.ones((8, LANES), BF16)

    def compute(t, s):
        h_row = res_ref[pl.ds(t, 1), :]
        ms = jnp.mean(h_row * h_row, axis=-1, keepdims=True)
        x = (h_row * lax.rsqrt(ms + RMS_EPS) * ng_ref[...]) * (1.0 + sc_ref[...]) + sh_ref[...]
        p, vs = None, []
        for q in range(sub):
            word = slots[s][q]
            u_q = lax.bitcast_convert_type(word << 16, F32)
            vs.append(lax.bitcast_convert_type(word & jnp.uint32(0xFFFF0000), F32))
            term = u_q * x[:, q * LANES:(q + 1) * LANES]
            p = term if p is None else p + term
        hi = p.astype(BF16)
        lo = (p - hi.astype(F32)).astype(BF16)
        act = (lax.dot_general(ones, hi, NT_DIMS, preferred_element_type=F32)
               + lax.dot_general(ones, lo, NT_DIMS, preferred_element_type=F32))
        gelu = 0.5 * act * (1.0 + lax.erf(act * (2.0 ** -0.5)))
        w = (wg_ref[pl.ds(t, 1), :] * gelu).astype(BF16)
        v = jnp.concatenate(vs, axis=1).astype(BF16)
        out = jnp.dot(w, v, preferred_element_type=F32)
        o_ref[pl.ds(t, 1), :] = h_row + g2_ref[...] * out[0:1]

    @pl.when(step == 0)
    def _():
        first = idx_copy(0, 0)
        first.start()
        first.wait()
        for r in range(GATHER_RING - 1):
            issue_group(r, r)

    @pl.when(step < last_step)
    def _():
        idx_copy(step + 1, nxt).start()

    n_sweeps = tb // GATHER_SLOTS

    def sweep(it, carry):
        @pl.when(jnp.logical_and(it == n_sweeps - 1, step < last_step))
        def _():
            idx_copy(step + 1, nxt).wait()

        for r in range(GATHER_RING):
            g = it * GATHER_RING + r
            wait_group(r)
            issue_group(g + GATHER_RING - 1, (r - 1) % GATHER_RING)
            for k in range(GATHER_GROUP):
                compute(g * GATHER_GROUP + k, r * GATHER_GROUP + k)
        return carry

    lax.fori_loop(0, n_sweeps, sweep, 0)

    @pl.when(step == last_step)
    def _():
        for r in range(GATHER_RING - 1):
            wait_group(r)


def peer_experts(idx, gates, res, norm_g, shift, scale, gate2, uv):
    B, L, D = res.shape
    M = B * L
    n_sel = idx.shape[1]
    tb = _tile(L, 512)
    lb = L // tb
    assert tb % GATHER_SLOTS == 0
    body = functools.partial(_gather_body, n_sel=n_sel, d=D)
    tok = lambda shape_last: pl.BlockSpec((tb, shape_last), lambda i: (i, 0))
    vec = pl.BlockSpec((None, 1, D), lambda i: (i // lb, 0, 0))
    out = pl.pallas_call(
        body,
        grid=(M // tb,),
        in_specs=[pl.BlockSpec(memory_space=pl.ANY), tok(n_sel), tok(D),
                  pl.BlockSpec((1, D), lambda i: (0, 0)), vec, vec, vec,
                  pl.BlockSpec(memory_space=pl.ANY)],
        out_specs=tok(D),
        out_shape=jax.ShapeDtypeStruct((M, D), F32),
        scratch_shapes=([pltpu.SMEM((2 * tb * n_sel,), jnp.int32)]
                        + [pltpu.VMEM((D // LANES, n_sel, LANES), jnp.uint32)] * GATHER_SLOTS
                        + [pltpu.SemaphoreType.DMA(()), pltpu.SemaphoreType.DMA((GATHER_SLOTS,))]),
        compiler_params=_params("arbitrary"),
        name="peer_experts",
    )(idx.reshape(M // tb, tb * n_sel), gates, res.reshape(M, D), norm_g.reshape(1, D), shift, scale,
      gate2, uv)
    return out.reshape(B, L, D)


def peer_mix(h, g, shift, scale, gate2, w_q, keys1, keys2, uv):
    B, L, D = h.shape
    a = modulate(h, g, shift, scale)
    q = matmul(a, w_q, out_dtype=BF16)
    idx, gates = peer_route(q.reshape(B * L, -1), keys1, keys2)
    n_sel = idx.shape[0] * idx.shape[1]
    idx = idx.reshape(n_sel, B * L).T
    gates = gates.reshape(n_sel, B * L).T
    return peer_experts(idx, gates, h, g, shift, scale, gate2, uv)


def kernel(x, c, ctx, c_ctx, ada_w, ada_b, norm_mix_g, norm_ffn_g, final_g, attn_w_qkv, attn_w_o, attn_sink, hy_w_in, hy_conv_w, hy_conv_b, hy_f_w1, hy_f_b1, hy_f_w2, hy_f_b2, hy_f_w3, hy_decay, hy_fbias, hy_w_out, peer_w_q, peer_keys1, peer_keys2, peer_u, peer_v):
    B, L, D = x.shape
    C = ctx.shape[1]
    depth = ada_w.shape[0]
    n_mixers = 2
    last_attn = max(i for i in range(depth) if i % n_mixers == 0)
    nq = attn_w_o.shape[1] // HEAD_DIM
    nkv = (attn_w_qkv.shape[2] - nq * HEAD_DIM) // (2 * HEAD_DIM)

    cond = jnp.concatenate([jax.nn.silu(c), jax.nn.silu(c_ctx)[None]], axis=0)
    cond = jnp.pad(cond, ((0, -(B + 1) % 16), (0, 0))).astype(BF16)[None]
    rope = rope_tables(L)
    h_lat, h_ctx = x, ctx

    for i in range(depth):
        is_attn = i % n_mixers == 0
        j = i // n_mixers
        ctx_update = i < last_attn
        mods = matmul(cond, ada_w[i])[0, :B + 1] + ada_b[i]
        lat = [m[:, None, :] for m in jnp.split(mods[:B], N_MODS, axis=-1)]
        cx = [jnp.broadcast_to(m[None], (B, 1, D)) for m in jnp.split(mods[B:], N_MODS, axis=-1)]
        sh1, sc1, g1, sh2, sc2, g2 = lat
        csh1, csc1, cg1, csh2, csc2, cg2 = cx
        a_lat = modulate(h_lat, norm_mix_g[i], sh1, sc1)

        if is_attn:
            w_qkv, w_o = attn_w_qkv[j], attn_w_o[j]
            a_ctx = modulate(h_ctx, norm_mix_g[i], csh1, csc1)
            qkv = matmul(a_lat, w_qkv, mode="rope", out_dtype=BF16, rope=rope,
                         n_rope_cols=(nq + nkv) * HEAD_DIM)
            qkv_ctx = matmul(a_ctx, w_qkv, out_dtype=BF16)
            o_lat = attention(qkv, qkv_ctx, attn_sink[j], nq=nq, nkv=nkv, windowed=True)
            h_lat = matmul(o_lat, w_o, mode="res", res=h_lat, gate=g1)
            if ctx_update:
                o_ctx = attention(qkv_ctx, qkv_ctx, attn_sink[j], nq=nq, nkv=nkv, windowed=False)
                h_ctx = matmul(o_ctx, w_o, mode="res", res=h_ctx, gate=cg1)
        else:
            w_in, w_out = hy_w_in[j], hy_w_out[j]
            f_args = (hy_f_w1[j], hy_f_b1[j], hy_f_w2[j], hy_f_b2[j], hy_f_w3[j], hy_decay[j])
            seqs = [(a_lat, h_lat, g1)]
            if ctx_update:
                seqs.append((modulate(h_ctx, norm_mix_g[i], csh1, csc1), h_ctx, cg1))
            outs = []
            for a_seq, h_seq, gate in seqs:
                Ls = a_seq.shape[1]
                fwd, inv, _ = _dft_matrices(Ls)
                fwd, inv = jnp.asarray(fwd, BF16), jnp.asarray(inv, BF16)
                spec = hyena_spectra(Ls, fwd, hyena_filters(Ls, *f_args))
                outs.append(hyena_mix(a_seq, h_seq, gate, w_in, hy_conv_w[j], hy_conv_b[j],
                                      w_out, hy_fbias[j], (fwd, inv), spec))
            h_lat = outs[0]
            if ctx_update:
                h_ctx = outs[1]

        w_q = peer_w_q[i]
        uv = pack_expert_tables(peer_u[i], peer_v[i])
        h_lat = peer_mix(h_lat, norm_ffn_g[i], sh2, sc2, g2, w_q, peer_keys1[i], peer_keys2[i], uv)
        if ctx_update:
            h_ctx = peer_mix(h_ctx, norm_ffn_g[i], csh2, csc2, cg2, w_q, peer_keys1[i],
                             peer_keys2[i], uv)

    zero = jnp.zeros((B, 1, D), F32)
    return modulate(h_lat, final_g, zero, zero, out_dtypes=(F32,))
```

```python
import functools
import math

import numpy as np
import jax
import jax.numpy as jnp
from jax import lax
from jax.experimental import pallas as pl
from jax.experimental.pallas import tpu as pltpu

F32 = jnp.float32
BF16 = jnp.bfloat16

LANES = 128
HEAD_DIM = 128
GRID_W = 64
ATTN_BLOCK = 128
ROPE_BASE = 10000.0
RMS_EPS = 1e-6
NEG_INF = -1e30
N_MODS = 6
HY_EMB_BANDS = 16
HY_MOD_SHIFT = 0.05
PEER_TOPK = 16
VMEM_LIMIT = 56 * 2 ** 20
NT_DIMS = (((1,), (1,)), ((), ()))


def _params(*sem):
    return pltpu.CompilerParams(dimension_semantics=sem, vmem_limit_bytes=VMEM_LIMIT)


def _tile(n, pref):
    if n <= pref:
        return n
    t = (pref // LANES) * LANES
    while n % t:
        t -= LANES
    return t


def _modulate_body(h_ref, g_ref, sh_ref, sc_ref, *o_refs):
    x = h_ref[...]
    ms = jnp.mean(x * x, axis=-1, keepdims=True)
    y = x * lax.rsqrt(ms + RMS_EPS) * g_ref[...]
    y = y * (1.0 + sc_ref[...]) + sh_ref[...]
    for o_ref in o_refs:
        o_ref[...] = y.astype(o_ref.dtype)


def modulate(h, g, shift, scale, out_dtypes=None):
    B, L, D = h.shape
    out_dtypes = out_dtypes or (BF16,)
    tm = _tile(L, 512)
    vec = pl.BlockSpec((None, 1, D), lambda b, i: (b, 0, 0))
    blk = pl.BlockSpec((None, tm, D), lambda b, i: (b, i, 0))
    outs = pl.pallas_call(
        _modulate_body,
        grid=(B, L // tm),
        in_specs=[blk, pl.BlockSpec((1, D), lambda b, i: (0, 0)), vec, vec],
        out_specs=[blk] * len(out_dtypes),
        out_shape=[jax.ShapeDtypeStruct((B, L, D), dt) for dt in out_dtypes],
        compiler_params=_params("parallel", "parallel"),
        name="modulate",
    )(h, g.reshape(1, D), shift, scale)
    return outs[0] if len(outs) == 1 else outs


def _mm_plain_body(a_ref, w_ref, o_ref):
    acc = jnp.dot(a_ref[...], w_ref[...].astype(BF16), preferred_element_type=F32)
    o_ref[...] = acc.astype(o_ref.dtype)


def _mm_res_body(a_ref, w_ref, r_ref, g_ref, o_ref):
    acc = jnp.dot(a_ref[...], w_ref[...].astype(BF16), preferred_element_type=F32)
    o_ref[...] = r_ref[...] + g_ref[...] * acc


def _mm_rope_body(a_ref, w_ref, cos_ref, sa_ref, sb_ref, o_ref, *, n_rope):
    acc = jnp.dot(a_ref[...], w_ref[...].astype(BF16), preferred_element_type=F32)
    j = pl.program_id(2)

    @pl.when(j < n_rope)
    def _():
        cos, sa, sb = cos_ref[...], sa_ref[...], sb_ref[...]
        quarter = HEAD_DIM // 4
        for s in range(acc.shape[1] // HEAD_DIM):
            x = acc[:, s * HEAD_DIM:(s + 1) * HEAD_DIM]
            y = (x * cos + pltpu.roll(x, quarter, axis=1) * sa
                 + pltpu.roll(x, HEAD_DIM - quarter, axis=1) * sb)
            o_ref[:, s * HEAD_DIM:(s + 1) * HEAD_DIM] = y.astype(o_ref.dtype)

    @pl.when(j >= n_rope)
    def _():
        o_ref[...] = acc.astype(o_ref.dtype)


def _mm_conv3_body(a_ref, w_ref, cw_ref, cb_ref, o_ref):
    y = jnp.dot(a_ref[...], w_ref[...].astype(BF16), preferred_element_type=F32)
    L = y.shape[0]
    row = lax.broadcasted_iota(jnp.int32, y.shape, 0)
    prev = jnp.where(row == 0, 0.0, pltpu.roll(y, 1, axis=0))
    nxt = jnp.where(row == L - 1, 0.0, pltpu.roll(y, L - 1, axis=0))
    o_ref[...] = cb_ref[...] + prev * cw_ref[0:1, :] + y * cw_ref[1:2, :] + nxt * cw_ref[2:3, :]


def matmul(a, w, *, mode="plain", out_dtype=F32, tm=2048, tn=512, res=None, gate=None,
           rope=None, n_rope_cols=0, conv_w=None, conv_b=None):
    B, L, K = a.shape
    N = w.shape[1]
    tm = L if mode == "conv3" else _tile(L, tm)
    if mode == "rope":
        tn = _tile(math.gcd(N, n_rope_cols), tn)
    else:
        tn = _tile(N, 256 if mode == "conv3" else tn)
    grid = (B, L // tm, N // tn)
    a_spec = pl.BlockSpec((None, tm, K), lambda b, i, j: (b, i, 0))
    w_spec = pl.BlockSpec((K, tn), lambda b, i, j: (0, j))
    o_spec = pl.BlockSpec((None, tm, tn), lambda b, i, j: (b, i, j))
    in_specs, args = [a_spec, w_spec], [a, w]
    if mode == "plain":
        body = _mm_plain_body
    elif mode == "res":
        body, out_dtype = _mm_res_body, F32
        in_specs += [o_spec, pl.BlockSpec((None, 1, tn), lambda b, i, j: (b, 0, j))]
        args += [res, gate]
    elif mode == "rope":
        assert n_rope_cols % tn == 0
        body = functools.partial(_mm_rope_body, n_rope=n_rope_cols // tn)
        tab = pl.BlockSpec((tm, HEAD_DIM), lambda b, i, j: (i, 0))
        in_specs += [tab, tab, tab]
        args += list(rope)
    elif mode == "conv3":
        body, out_dtype = _mm_conv3_body, F32
        in_specs += [pl.BlockSpec((3, tn), lambda b, i, j: (0, j)),
                     pl.BlockSpec((1, tn), lambda b, i, j: (0, j))]
        args += [conv_w, conv_b.reshape(1, N)]
    else:
        raise ValueError(mode)
    return pl.pallas_call(
        body, grid=grid, in_specs=in_specs, out_specs=o_spec,
        out_shape=jax.ShapeDtypeStruct((B, L, N), out_dtype),
        compiler_params=_params("parallel", "parallel", "arbitrary"),
        name="mm_" + mode,
    )(*args)


def _attn_body(sink_ref, q_ref, *refs, nkv, grp, nb, windowed):
    if windowed:
        kp_ref, kc_ref, kn_ref, kx_ref, o_ref = refs
    else:
        kx_ref, o_ref = refs
    n = pl.program_id(1)
    scale = HEAD_DIM ** -0.5
    rows = grp * ATTN_BLOCK
    a = lax.broadcasted_iota(jnp.int32, (rows, ATTN_BLOCK), 0) % ATTN_BLOCK
    kb = lax.broadcasted_iota(jnp.int32, (rows, ATTN_BLOCK), 1)

    def head_cols(ref, c):
        return ref[:, c * HEAD_DIM:(c + 1) * HEAD_DIM]

    for h in range(nkv):
        qh = jnp.concatenate([head_cols(q_ref, h * grp + g) for g in range(grp)], axis=0)
        sink = jnp.concatenate(
            [jnp.full((ATTN_BLOCK, 1), sink_ref[h * grp + g], F32) for g in range(grp)], axis=0)

        def scores(ref):
            return lax.dot_general(qh, head_cols(ref, h), NT_DIMS, preferred_element_type=F32) * scale

        pieces = []
        if windowed:
            s_p = jnp.where(jnp.logical_and(kb >= a, n > 0), scores(kp_ref), NEG_INF)
            s_n = jnp.where(jnp.logical_and(kb <= a, n < nb - 1), scores(kn_ref), NEG_INF)
            pieces += [(s_p, kp_ref), (scores(kc_ref), kc_ref), (s_n, kn_ref)]
        pieces.append((scores(kx_ref), kx_ref))

        def lane_blocks(s):
            return [s[:, c:c + LANES] for c in range(0, s.shape[1], LANES)]

        mx = functools.reduce(jnp.maximum, [blk for s, _ in pieces for blk in lane_blocks(s)])
        m = jnp.maximum(sink, jnp.max(mx, axis=-1, keepdims=True))
        psum = jnp.zeros((rows, LANES), F32)
        o = jnp.zeros((rows, HEAD_DIM), F32)
        for s, ref in pieces:
            p = jnp.exp(s - m)
            psum = functools.reduce(jnp.add, lane_blocks(p), psum)
            o = o + jnp.dot(p.astype(BF16), head_cols(ref, nkv + h), preferred_element_type=F32)
        o = o / (jnp.exp(sink - m) + jnp.sum(psum, axis=-1, keepdims=True))
        for g in range(grp):
            c = h * grp + g
            o_ref[:, c * HEAD_DIM:(c + 1) * HEAD_DIM] = (
                o[g * ATTN_BLOCK:(g + 1) * ATTN_BLOCK].astype(o_ref.dtype))


def attention(qkv, qkv_ctx, sink, *, nq, nkv, windowed):
    B, L, _ = qkv.shape
    C = qkv_ctx.shape[1]
    nb = L // ATTN_BLOCK
    qd, kvd = nq * HEAD_DIM, nkv * HEAD_DIM
    assert qd % (2 * kvd) == 0 and C % LANES == 0
    kv_col = qd // (2 * kvd)
    q_spec = pl.BlockSpec((None, ATTN_BLOCK, qd), lambda b, n, s: (b, n, 0))

    def kv_spec(off):
        return pl.BlockSpec(
            (None, ATTN_BLOCK, 2 * kvd),
            lambda b, n, s: (b, jnp.clip(n + off, 0, nb - 1), kv_col))

    ctx_spec = pl.BlockSpec((None, C, 2 * kvd), lambda b, n, s: (b, 0, kv_col))
    if windowed:
        in_specs = [q_spec, kv_spec(-1), kv_spec(0), kv_spec(1), ctx_spec]
        args = [qkv, qkv, qkv, qkv, qkv_ctx]
    else:
        in_specs, args = [q_spec, ctx_spec], [qkv, qkv_ctx]
    body = functools.partial(_attn_body, nkv=nkv, grp=nq // nkv, nb=nb, windowed=windowed)
    return pl.pallas_call(
        body,
        grid_spec=pltpu.PrefetchScalarGridSpec(
            num_scalar_prefetch=1, grid=(B, nb), in_specs=in_specs,
            out_specs=pl.BlockSpec((None, ATTN_BLOCK, qd), lambda b, n, s: (b, n, 0))),
        out_shape=jax.ShapeDtypeStruct((B, L, qd), BF16),
        compiler_params=_params("parallel", "parallel"),
        name="attn_win" if windowed else "attn_ctx",
    )(sink.astype(F32), *args)


def rope_tables(L):
    rows = L // GRID_W
    row = jnp.repeat(jnp.arange(rows), GRID_W).astype(F32)
    col = jnp.tile(jnp.arange(GRID_W), rows).astype(F32)
    n_freq = HEAD_DIM // 4
    inv = ROPE_BASE ** (-jnp.arange(n_freq, dtype=F32) / n_freq)
    ar, ac = row[:, None] * inv, col[:, None] * inv
    ang = jnp.concatenate([ar, ar, ac, ac], axis=-1)
    cos, sin = jnp.cos(ang), jnp.sin(ang)
    odd = (jnp.arange(HEAD_DIM) // n_freq) % 2 == 1
    return cos, jnp.where(odd, sin, 0.0), jnp.where(odd, 0.0, -sin)


@functools.lru_cache(maxsize=None)
def _dft_matrices(L):
    N = 2 * L
    fb = min(256, L)
    nfb = L // fb
    k = np.arange(L, dtype=np.int64)[:, None]
    t = np.arange(L, dtype=np.int64)[None, :]
    ang = 2.0 * np.pi * ((k * t) % N).astype(np.float64) / N
    c, s = np.cos(ang), np.sin(ang)
    nyq = np.where(np.arange(L) % 2 == 0, 1.0, -1.0)
    f_re, f_im = c.copy(), -s
    f_im[0] = nyq
    i_re, i_im = 2.0 * c / N, -2.0 * s / N
    i_re[0] = 1.0 / N
    i_im[0] = nyq / N
    fwd = np.concatenate([f_re.reshape(nfb, fb, L), f_im.reshape(nfb, fb, L)], axis=1)
    inv = np.concatenate([i_re.reshape(nfb, fb, L), i_im.reshape(nfb, fb, L)], axis=1)
    inv = np.transpose(inv, (0, 2, 1))
    return fwd.astype(np.float32), inv.astype(np.float32), fb


def _filter_body(z_ref, w1_ref, b1_ref, w2_ref, b2_ref, w3_ref, dec_ref, o_ref, *, n_inner):
    hi = lax.Precision.HIGHEST
    h = jnp.sin(jnp.dot(z_ref[...], w1_ref[...], precision=hi, preferred_element_type=F32)
                + b1_ref[...])
    for i in range(n_inner):
        h = jnp.sin(jnp.dot(h, w2_ref[i], precision=hi, preferred_element_type=F32)
                    + b2_ref[i])
    y = jnp.dot(h, w3_ref[...], precision=hi, preferred_element_type=F32)
    L = y.shape[0]
    t01 = lax.broadcasted_iota(jnp.int32, y.shape, 0).astype(F32) / L
    o_ref[...] = y * (jnp.exp(-t01 * jnp.abs(dec_ref[...])) + HY_MOD_SHIFT)


def hyena_filters(L, f_w1, f_b1, f_w2, f_b2, f_w3, decay):
    order, D = decay.shape
    hid = f_w1.shape[1]
    n_inner = f_w2.shape[0]
    n_dir = f_w3.shape[1] // (order * D)
    t = np.arange(L, dtype=np.float32)
    bands = np.arange(1, HY_EMB_BANDS + 1, dtype=np.float32)
    ang = (np.float32(2.0 * math.pi) * t[:, None] * bands[None, :] / np.float32(L)).astype(np.float32)
    z = np.concatenate([(t / np.float32(L))[:, None], np.cos(ang), np.sin(ang)], axis=-1)
    emb = z.shape[1]
    emb_pad = -(-emb // 8) * 8
    z = np.pad(z, ((0, 0), (0, emb_pad - emb))).astype(np.float32)
    w1 = jnp.pad(f_w1, ((0, emb_pad - emb), (0, 0)))
    tn = _tile(D, 1024)
    per_o = n_dir * D // tn
    full = lambda shape: pl.BlockSpec(shape, lambda j: (0,) * len(shape))
    out = pl.pallas_call(
        functools.partial(_filter_body, n_inner=n_inner),
        grid=(order * n_dir * D // tn,),
        in_specs=[full((L, emb_pad)), full((emb_pad, hid)), full((1, hid)),
                  full((n_inner, hid, hid)), full((n_inner, 1, hid)),
                  pl.BlockSpec((hid, tn), lambda j: (0, j)),
                  pl.BlockSpec((None, 1, tn), lambda j: (j // per_o, 0, j % (D // tn)))],
        out_specs=pl.BlockSpec((L, tn), lambda j: (0, j)),
        out_shape=jax.ShapeDtypeStruct((L, order * n_dir * D), F32),
        compiler_params=_params("parallel"),
        name="hyena_filter",
    )(jnp.asarray(z), w1, f_b1.reshape(1, hid), f_w2, f_b2.reshape(n_inner, 1, hid), f_w3,
      decay.reshape(order, 1, D))
    return out.reshape(L, order, n_dir, D)


def hyena_spectra(L, fwd, filt):
    order, D = filt.shape[1], filt.shape[3]
    nfb, fb2, _ = fwd.shape
    f = filt[:, :, 0]
    g = filt[:, :, 1].at[0].set(0.0)
    cols = jnp.stack([f + g, f - g], axis=2).reshape(L, order * 2 * D)
    hi = cols.astype(BF16)
    lo = (cols - hi.astype(F32)).astype(BF16)
    fwd2 = jnp.concatenate([fwd, fwd], axis=-1).reshape(1, nfb * fb2, 2 * L)
    spec = matmul(fwd2, jnp.concatenate([hi, lo], axis=0), tm=1024)
    return spec.reshape(nfb, fb2, order * 2 * D)


def _lconv_body(v_ref, x_ref, fw_ref, iv_ref, ha_ref, hb_ref, nyq_ref, bias_ref, o_ref,
                acc_ref, vb_ref, *, fb):
    f = pl.program_id(2)

    @pl.when(f == 0)
    def _():
        vb_ref[...] = v_ref[...].astype(BF16)
        acc_ref[...] = jnp.zeros_like(acc_ref)

    spec = jnp.dot(fw_ref[...], vb_ref[...], preferred_element_type=F32)
    xr, xi = spec[:fb], spec[fb:]
    a = ha_ref[...]
    dc = jnp.logical_and(lax.broadcasted_iota(jnp.int32, a.shape, 0) == 0, f == 0)
    b = jnp.where(dc, 0.0, hb_ref[...])
    c = jnp.where(dc, nyq_ref[0:1, :], a)
    yr = (xr * a - xi * b).astype(BF16)
    yi = (xr * b + xi * c).astype(BF16)
    acc_ref[...] += (jnp.dot(iv_ref[:, :fb], yr, preferred_element_type=F32)
                     + jnp.dot(iv_ref[:, fb:], yi, preferred_element_type=F32))

    @pl.when(f == pl.num_programs(2) - 1)
    def _():
        o_ref[...] = (x_ref[...] * (acc_ref[...] + v_ref[...] * bias_ref[...])).astype(o_ref.dtype)


def long_conv_gated(src, src_blk, gate, gate_blk, fwd, inv, spec, o, bias, out_dtype):
    B, L, _ = src.shape
    D = bias.shape[0]
    nfb, fb2, _ = fwd.shape
    fb = fb2 // 2
    tc = _tile(D, 512)
    ncb = D // tc
    col = lambda blk: pl.BlockSpec((None, L, tc), lambda c, b, f: (b, 0, blk * ncb + c))

    def hspec(part, grp, fixed_f=None):
        return pl.BlockSpec((None, fb, tc), lambda c, b, f: (f if fixed_f is None else fixed_f,
                                                            part, grp * ncb + c))

    return pl.pallas_call(
        functools.partial(_lconv_body, fb=fb),
        grid=(ncb, B, nfb),
        in_specs=[col(src_blk), col(gate_blk),
                  pl.BlockSpec((None, fb2, L), lambda c, b, f: (f, 0, 0)),
                  pl.BlockSpec((None, L, fb2), lambda c, b, f: (f, 0, 0)),
                  hspec(0, 2 * o), hspec(1, 2 * o + 1), hspec(1, 2 * o, fixed_f=0),
                  pl.BlockSpec((1, tc), lambda c, b, f: (0, c))],
        out_specs=pl.BlockSpec((None, L, tc), lambda c, b, f: (b, 0, c)),
        out_shape=jax.ShapeDtypeStruct((B, L, D), out_dtype),
        scratch_shapes=[pltpu.VMEM((L, tc), F32), pltpu.VMEM((L, tc), BF16)],
        compiler_params=_params("parallel", "parallel", "arbitrary"),
        name="long_conv",
    )(src, gate, fwd, inv, spec, spec, spec, bias.reshape(1, D))


def hyena_mix(a, res, gate, w_in, conv_w, conv_b, w_out, fbias, dft, spec):
    fwd, inv = dft
    u = matmul(a, w_in, mode="conv3", conv_w=conv_w, conv_b=conv_b)
    z = long_conv_gated(u, 0, u, 1, fwd, inv, spec, 0, fbias[0], F32)
    y = long_conv_gated(z, 0, u, 2, fwd, inv, spec, 1, fbias[1], BF16)
    return matmul(y, w_out, mode="res", res=res, gate=gate)


def _topk_rows(s, k, key=None):
    if key is None:
        key = lax.broadcasted_iota(jnp.int32, s.shape, 0)
    big = jnp.iinfo(jnp.int32).max
    vals, keys = [], []
    for _ in range(k):
        m = jnp.max(s, axis=0, keepdims=True)
        i = jnp.min(jnp.where(s == m, key, big), axis=0, keepdims=True)
        vals.append(m)
        keys.append(i)
        s = jnp.where(key == i, -jnp.inf, s)
    return vals, keys


def _route_body(q_ref, k1_ref, k2_ref, idx_ref, g_ref, *, half, n_keys):
    T = q_ref.shape[0]
    K = PEER_TOPK
    assert K == 16
    r8 = lax.broadcasted_iota(jnp.int32, (8, LANES), 0)
    r16 = lax.broadcasted_iota(jnp.int32, (K, LANES), 0)
    for c in range(T // LANES):
        q = q_ref[c * LANES:(c + 1) * LANES, :]
        s1 = lax.dot_general(k1_ref[...], q[:, :half], NT_DIMS, preferred_element_type=F32)
        s2 = lax.dot_general(k2_ref[...], q[:, half:], NT_DIMS, preferred_element_type=F32)
        v1, i1 = _topk_rows(s1, K)
        v2, i2 = _topk_rows(s2, K)
        v1c, i1c = jnp.concatenate(v1, axis=0), jnp.concatenate(i1, axis=0)
        v2c, i2c = jnp.concatenate(v2, axis=0), jnp.concatenate(i2, axis=0)
        cand = [v1[0] + v2c]
        cid = [i1[0] * n_keys + i2c]
        pos = [r16]
        for i in range(1, 8):
            n = K // (i + 1)
            val = v1[i] + v2c[:8]
            cand.append(val if n >= 8 else jnp.where(r8 < n, val, -jnp.inf))
            cid.append(i1[i] * n_keys + i2c[:8])
            pos.append(i * K + r8)
        cand.append(v1c[8:] + v2[0])
        cid.append(i1c[8:] * n_keys + i2[0])
        pos.append((8 + r8) * K)
        cand, cid, pos = (jnp.concatenate(x, axis=0) for x in (cand, cid, pos))
        top, sel = _topk_rows(cand, K, key=pos)
        ids = [jnp.max(jnp.where(pos == p, cid, -1), axis=0, keepdims=True) for p in sel]
        e = [jnp.exp(t - top[0]) for t in top]
        den = e[0]
        for x in e[1:]:
            den = den + x
        idx_ref[:, c * LANES:(c + 1) * LANES] = jnp.concatenate(ids, axis=0)
        g_ref[:, c * LANES:(c + 1) * LANES] = jnp.concatenate(e, axis=0) / den


def peer_route(q, keys1, keys2):
    M = q.shape[0]
    H, n_keys, half = keys1.shape
    T = _tile(M, 512)
    kspec = pl.BlockSpec((None, n_keys, half), lambda i, h: (h, 0, 0))
    ospec = pl.BlockSpec((None, PEER_TOPK, T), lambda i, h: (h, 0, i))
    return pl.pallas_call(
        functools.partial(_route_body, half=half, n_keys=n_keys),
        grid=(M // T, H),
        in_specs=[pl.BlockSpec((T, 2 * half), lambda i, h: (i, h)), kspec, kspec],
        out_specs=[ospec, ospec],
        out_shape=[jax.ShapeDtypeStruct((H, PEER_TOPK, M), jnp.int32),
                   jax.ShapeDtypeStruct((H, PEER_TOPK, M), F32)],
        compiler_params=_params("parallel", "arbitrary"),
        name="peer_route",
    )(q, keys1.astype(BF16), keys2.astype(BF16))


GATHER_GROUP = 8
GATHER_RING = 4
GATHER_SLOTS = GATHER_GROUP * GATHER_RING


def pack_expert_tables(u, v):
    E, D = u.shape
    ub = lax.bitcast_convert_type(u.astype(jnp.bfloat16), jnp.uint16).astype(jnp.uint32)
    vb = lax.bitcast_convert_type(v.astype(jnp.bfloat16), jnp.uint16).astype(jnp.uint32)
    return (ub | (vb << 16)).reshape(E, D // LANES, LANES)


def _gather_body(idx_hbm, wg_ref, x_ref, res_ref, g2_ref, uv_hbm, o_ref, idx_smem, *scratch,
                 n_sel, d):
    slots, (isem, sems) = scratch[:GATHER_SLOTS], scratch[GATHER_SLOTS:]
    sub = d // LANES
    tb = x_ref.shape[0]
    n_idx = tb * n_sel
    step = pl.program_id(0)
    last_step = pl.num_programs(0) - 1
    cur = (step % 2) * n_idx
    nxt = n_idx - cur

    def idx_copy(blk, off):
        return pltpu.make_async_copy(idx_hbm.at[blk], idx_smem.at[pl.ds(off, n_idx)], isem)

    def row_copy(e, j, s):
        return pltpu.make_async_copy(uv_hbm.at[e], slots[s].at[:, j, :], sems.at[s])

    def issue_group(g, ring):
        for k in range(GATHER_GROUP):
            t = g * GATHER_GROUP + k
            beyond = jnp.where(step == last_step, cur + (tb - 1) * n_sel, nxt + (t - tb) * n_sel)
            base = jnp.where(t < tb, cur + t * n_sel, beyond)
            for j in range(n_sel):
                row_copy(idx_smem[base + j], j, ring * GATHER_GROUP + k).start(priority=j % 2)

    def wait_group(ring):
        for k in range(GATHER_GROUP):
            for j in range(n_sel):
                row_copy(0, j, ring * GATHER_GROUP + k).wait()

    ones = jnp.ones((8, LANES), BF16)

    def compute(t, s):
        x = x_ref[pl.ds(t, 1), :]
        p, vs = None, []
        for q in range(sub):
            word = slots[s][q]
            u_q = lax.bitcast_convert_type(word << 16, F32)
            vs.append(lax.bitcast_convert_type(word & jnp.uint32(0xFFFF0000), F32))
            term = u_q * x[:, q * LANES:(q + 1) * LANES]
            p = term if p is None else p + term
        hi = p.astype(BF16)
        lo = (p - hi.astype(F32)).astype(BF16)
        act = (lax.dot_general(ones, hi, NT_DIMS, preferred_element_type=F32)
               + lax.dot_general(ones, lo, NT_DIMS, preferred_element_type=F32))
        gelu = 0.5 * act * (1.0 + lax.erf(act * (2.0 ** -0.5)))
        w = (wg_ref[pl.ds(t, 1), :] * gelu).astype(BF16)
        v = jnp.concatenate(vs, axis=1).astype(BF16)
        out = jnp.dot(w, v, preferred_element_type=F32)
        o_ref[pl.ds(t, 1), :] = res_ref[pl.ds(t, 1), :] + g2_ref[...] * out[0:1]

    @pl.when(step == 0)
    def _():
        first = idx_copy(0, 0)
        first.start()
        first.wait()
        for r in range(GATHER_RING - 1):
            issue_group(r, r)

    @pl.when(step < last_step)
    def _():
        idx_copy(step + 1, nxt).start()

    n_sweeps = tb // GATHER_SLOTS

    def sweep(it, carry):
        @pl.when(jnp.logical_and(it == n_sweeps - 1, step < last_step))
        def _():
            idx_copy(step + 1, nxt).wait()

        for r in range(GATHER_RING):
            g = it * GATHER_RING + r
            wait_group(r)
            issue_group(g + GATHER_RING - 1, (r - 1) % GATHER_RING)
            for k in range(GATHER_GROUP):
                compute(g * GATHER_GROUP + k, r * GATHER_GROUP + k)
        return carry

    lax.fori_loop(0, n_sweeps, sweep, 0)

    @pl.when(step == last_step)
    def _():
        for r in range(GATHER_RING - 1):
            wait_group(r)


def peer_experts(idx, gates, x, res, gate2, uv):
    B, L, D = x.shape
    M = B * L
    n_sel = idx.shape[1]
    tb = _tile(L, 256)
    lb = L // tb
    assert tb % GATHER_SLOTS == 0
    body = functools.partial(_gather_body, n_sel=n_sel, d=D)
    tok = lambda shape_last: pl.BlockSpec((tb, shape_last), lambda i: (i, 0))
    out = pl.pallas_call(
        body,
        grid=(M // tb,),
        in_specs=[pl.BlockSpec(memory_space=pl.ANY), tok(n_sel), tok(D), tok(D),
                  pl.BlockSpec((None, 1, D), lambda i: (i // lb, 0, 0)),
                  pl.BlockSpec(memory_space=pl.ANY)],
        out_specs=tok(D),
        out_shape=jax.ShapeDtypeStruct((M, D), F32),
        scratch_shapes=([pltpu.SMEM((2 * tb * n_sel,), jnp.int32)]
                        + [pltpu.VMEM((D // LANES, n_sel, LANES), jnp.uint32)] * GATHER_SLOTS
                        + [pltpu.SemaphoreType.DMA(()), pltpu.SemaphoreType.DMA((GATHER_SLOTS,))]),
        compiler_params=_params("arbitrary"),
        name="peer_experts",
    )(idx.reshape(M // tb, tb * n_sel), gates, x.reshape(M, D), res.reshape(M, D), gate2, uv)
    return out.reshape(B, L, D)


def peer_mix(h, g, shift, scale, gate2, w_q, keys1, keys2, uv):
    B, L, D = h.shape
    a, a32 = modulate(h, g, shift, scale, out_dtypes=(BF16, F32))
    q = matmul(a, w_q, out_dtype=BF16)
    idx, gates = peer_route(q.reshape(B * L, -1), keys1, keys2)
    n_sel = idx.shape[0] * idx.shape[1]
    idx = idx.reshape(n_sel, B * L).T
    gates = gates.reshape(n_sel, B * L).T
    return peer_experts(idx, gates, a32, h, gate2, uv)


def kernel(x, c, ctx, c_ctx, ada_w, ada_b, norm_mix_g, norm_ffn_g, final_g, attn_w_qkv, attn_w_o, attn_sink, hy_w_in, hy_conv_w, hy_conv_b, hy_f_w1, hy_f_b1, hy_f_w2, hy_f_b2, hy_f_w3, hy_decay, hy_fbias, hy_w_out, peer_w_q, peer_keys1, peer_keys2, peer_u, peer_v):
    B, L, D = x.shape
    C = ctx.shape[1]
    depth = ada_w.shape[0]
    n_mixers = 2
    last_attn = max(i for i in range(depth) if i % n_mixers == 0)
    nq = attn_w_o.shape[1] // HEAD_DIM
    nkv = (attn_w_qkv.shape[2] - nq * HEAD_DIM) // (2 * HEAD_DIM)

    cond = jnp.concatenate([jax.nn.silu(c), jax.nn.silu(c_ctx)[None]], axis=0)
    cond = jnp.pad(cond, ((0, -(B + 1) % 16), (0, 0))).astype(BF16)[None]
    rope = rope_tables(L)
    h_lat, h_ctx = x, ctx

    for i in range(depth):
        is_attn = i % n_mixers == 0
        j = i // n_mixers
        ctx_update = i < last_attn
        mods = matmul(cond, ada_w[i])[0, :B + 1] + ada_b[i]
        lat = [m[:, None, :] for m in jnp.split(mods[:B], N_MODS, axis=-1)]
        cx = [jnp.broadcast_to(m[None], (B, 1, D)) for m in jnp.split(mods[B:], N_MODS, axis=-1)]
        sh1, sc1, g1, sh2, sc2, g2 = lat
        csh1, csc1, cg1, csh2, csc2, cg2 = cx
        a_lat = modulate(h_lat, norm_mix_g[i], sh1, sc1)

        if is_attn:
            w_qkv, w_o = attn_w_qkv[j], attn_w_o[j]
            a_ctx = modulate(h_ctx, norm_mix_g[i], csh1, csc1)
            qkv = matmul(a_lat, w_qkv, mode="rope", out_dtype=BF16, rope=rope,
                         n_rope_cols=(nq + nkv) * HEAD_DIM)
            qkv_ctx = matmul(a_ctx, w_qkv, out_dtype=BF16)
            o_lat = attention(qkv, qkv_ctx, attn_sink[j], nq=nq, nkv=nkv, windowed=True)
            h_lat = matmul(o_lat, w_o, mode="res", res=h_lat, gate=g1)
            if ctx_update:
                o_ctx = attention(qkv_ctx, qkv_ctx, attn_sink[j], nq=nq, nkv=nkv, windowed=False)
                h_ctx = matmul(o_ctx, w_o, mode="res", res=h_ctx, gate=cg1)
        else:
            w_in, w_out = hy_w_in[j], hy_w_out[j]
            f_args = (hy_f_w1[j], hy_f_b1[j], hy_f_w2[j], hy_f_b2[j], hy_f_w3[j], hy_decay[j])
            seqs = [(a_lat, h_lat, g1)]
            if ctx_update:
                seqs.append((modulate(h_ctx, norm_mix_g[i], csh1, csc1), h_ctx, cg1))
            outs = []
            for a_seq, h_seq, gate in seqs:
                Ls = a_seq.shape[1]
                fwd, inv, _ = _dft_matrices(Ls)
                fwd, inv = jnp.asarray(fwd, BF16), jnp.asarray(inv, BF16)
                spec = hyena_spectra(Ls, fwd, hyena_filters(Ls, *f_args))
                outs.append(hyena_mix(a_seq, h_seq, gate, w_in, hy_conv_w[j], hy_conv_b[j],
                                      w_out, hy_fbias[j], (fwd, inv), spec))
            h_lat = outs[0]
            if ctx_update:
                h_ctx = outs[1]

        w_q = peer_w_q[i]
        uv = pack_expert_tables(peer_u[i], peer_v[i])
        h_lat = peer_mix(h_lat, norm_ffn_g[i], sh2, sc2, g2, w_q, peer_keys1[i], peer_keys2[i], uv)
        if ctx_update:
            h_ctx = peer_mix(h_ctx, norm_ffn_g[i], csh2, csc2, cg2, w_q, peer_keys1[i],
                             peer_keys2[i], uv)

    zero = jnp.zeros((B, 1, D), F32)
    return modulate(h_lat, final_g, zero, zero, out_dtypes=(F32,))
```

```python
import functools
import math

import numpy as np
import jax
import jax.numpy as jnp
from jax import lax
from jax.experimental import pallas as pl
from jax.experimental.pallas import tpu as pltpu

F32 = jnp.float32
BF16 = jnp.bfloat16

LANES = 128
HEAD_DIM = 128
GRID_W = 64
ATTN_BLOCK = 128
ROPE_BASE = 10000.0
RMS_EPS = 1e-6
NEG_INF = -1e30
N_MODS = 6
HY_EMB_BANDS = 16
HY_MOD_SHIFT = 0.05
PEER_TOPK = 16
VMEM_LIMIT = 56 * 2 ** 20
NT_DIMS = (((1,), (1,)), ((), ()))


def _params(*sem):
    return pltpu.CompilerParams(dimension_semantics=sem, vmem_limit_bytes=VMEM_LIMIT)


def _tile(n, pref):
    if n <= pref:
        return n
    t = (pref // LANES) * LANES
    while n % t:
        t -= LANES
    return t


def _modulate_body(h_ref, g_ref, sh_ref, sc_ref, *o_refs):
    x = h_ref[...]
    ms = jnp.mean(x * x, axis=-1, keepdims=True)
    y = x * lax.rsqrt(ms + RMS_EPS) * g_ref[...]
    y = y * (1.0 + sc_ref[...]) + sh_ref[...]
    for o_ref in o_refs:
        o_ref[...] = y.astype(o_ref.dtype)


def modulate(h, g, shift, scale, out_dtypes=None):
    B, L, D = h.shape
    out_dtypes = out_dtypes or (BF16,)
    tm = _tile(L, 512)
    vec = pl.BlockSpec((None, 1, D), lambda b, i: (b, 0, 0))
    blk = pl.BlockSpec((None, tm, D), lambda b, i: (b, i, 0))
    outs = pl.pallas_call(
        _modulate_body,
        grid=(B, L // tm),
        in_specs=[blk, pl.BlockSpec((1, D), lambda b, i: (0, 0)), vec, vec],
        out_specs=[blk] * len(out_dtypes),
        out_shape=[jax.ShapeDtypeStruct((B, L, D), dt) for dt in out_dtypes],
        compiler_params=_params("parallel", "parallel"),
        name="modulate",
    )(h, g.reshape(1, D), shift, scale)
    return outs[0] if len(outs) == 1 else outs


def _mm_plain_body(a_ref, w_ref, o_ref):
    acc = jnp.dot(a_ref[...], w_ref[...].astype(BF16), preferred_element_type=F32)
    o_ref[...] = acc.astype(o_ref.dtype)


def _mm_res_body(a_ref, w_ref, r_ref, g_ref, o_ref):
    acc = jnp.dot(a_ref[...], w_ref[...].astype(BF16), preferred_element_type=F32)
    o_ref[...] = r_ref[...] + g_ref[...] * acc


def _mm_rope_body(a_ref, w_ref, cos_ref, sa_ref, sb_ref, o_ref, *, n_rope):
    acc = jnp.dot(a_ref[...], w_ref[...].astype(BF16), preferred_element_type=F32)
    j = pl.program_id(2)

    @pl.when(j < n_rope)
    def _():
        cos, sa, sb = cos_ref[...], sa_ref[...], sb_ref[...]
        quarter = HEAD_DIM // 4
        for s in range(acc.shape[1] // HEAD_DIM):
            x = acc[:, s * HEAD_DIM:(s + 1) * HEAD_DIM]
            y = (x * cos + pltpu.roll(x, quarter, axis=1) * sa
                 + pltpu.roll(x, HEAD_DIM - quarter, axis=1) * sb)
            o_ref[:, s * HEAD_DIM:(s + 1) * HEAD_DIM] = y.astype(o_ref.dtype)

    @pl.when(j >= n_rope)
    def _():
        o_ref[...] = acc.astype(o_ref.dtype)


def _mm_conv3_body(a_ref, w_ref, cw_ref, cb_ref, o_ref):
    y = jnp.dot(a_ref[...], w_ref[...].astype(BF16), preferred_element_type=F32)
    L = y.shape[0]
    row = lax.broadcasted_iota(jnp.int32, y.shape, 0)
    prev = jnp.where(row == 0, 0.0, pltpu.roll(y, 1, axis=0))
    nxt = jnp.where(row == L - 1, 0.0, pltpu.roll(y, L - 1, axis=0))
    o_ref[...] = cb_ref[...] + prev * cw_ref[0:1, :] + y * cw_ref[1:2, :] + nxt * cw_ref[2:3, :]


def matmul(a, w, *, mode="plain", out_dtype=F32, tm=2048, tn=512, res=None, gate=None,
           rope=None, n_rope_cols=0, conv_w=None, conv_b=None):
    B, L, K = a.shape
    N = w.shape[1]
    tm = L if mode == "conv3" else _tile(L, tm)
    if mode == "rope":
        tn = _tile(math.gcd(N, n_rope_cols), tn)
    else:
        tn = _tile(N, 256 if mode == "conv3" else tn)
    grid = (B, L // tm, N // tn)
    a_spec = pl.BlockSpec((None, tm, K), lambda b, i, j: (b, i, 0))
    w_spec = pl.BlockSpec((K, tn), lambda b, i, j: (0, j))
    o_spec = pl.BlockSpec((None, tm, tn), lambda b, i, j: (b, i, j))
    in_specs, args = [a_spec, w_spec], [a, w]
    if mode == "plain":
        body = _mm_plain_body
    elif mode == "res":
        body, out_dtype = _mm_res_body, F32
        in_specs += [o_spec, pl.BlockSpec((None, 1, tn), lambda b, i, j: (b, 0, j))]
        args += [res, gate]
    elif mode == "rope":
        assert n_rope_cols % tn == 0
        body = functools.partial(_mm_rope_body, n_rope=n_rope_cols // tn)
        tab = pl.BlockSpec((tm, HEAD_DIM), lambda b, i, j: (i, 0))
        in_specs += [tab, tab, tab]
        args += list(rope)
    elif mode == "conv3":
        body, out_dtype = _mm_conv3_body, F32
        in_specs += [pl.BlockSpec((3, tn), lambda b, i, j: (0, j)),
                     pl.BlockSpec((1, tn), lambda b, i, j: (0, j))]
        args += [conv_w, conv_b.reshape(1, N)]
    else:
        raise ValueError(mode)
    return pl.pallas_call(
        body, grid=grid, in_specs=in_specs, out_specs=o_spec,
        out_shape=jax.ShapeDtypeStruct((B, L, N), out_dtype),
        compiler_params=_params("parallel", "parallel", "arbitrary"),
        name="mm_" + mode,
    )(*args)


def _attn_body(sink_ref, q_ref, *refs, nkv, grp, nb, windowed):
    if windowed:
        kp_ref, kc_ref, kn_ref, kx_ref, o_ref = refs
    else:
        kx_ref, o_ref = refs
    n = pl.program_id(1)
    scale = HEAD_DIM ** -0.5
    rows = grp * ATTN_BLOCK
    a = lax.broadcasted_iota(jnp.int32, (rows, ATTN_BLOCK), 0) % ATTN_BLOCK
    kb = lax.broadcasted_iota(jnp.int32, (rows, ATTN_BLOCK), 1)

    def head_cols(ref, c):
        return ref[:, c * HEAD_DIM:(c + 1) * HEAD_DIM]

    for h in range(nkv):
        qh = jnp.concatenate([head_cols(q_ref, h * grp + g) for g in range(grp)], axis=0)
        sink = jnp.concatenate(
            [jnp.full((ATTN_BLOCK, 1), sink_ref[h * grp + g], F32) for g in range(grp)], axis=0)

        def scores(ref):
            return lax.dot_general(qh, head_cols(ref, h), NT_DIMS, preferred_element_type=F32) * scale

        pieces = []
        if windowed:
            s_p = jnp.where(jnp.logical_and(kb >= a, n > 0), scores(kp_ref), NEG_INF)
            s_n = jnp.where(jnp.logical_and(kb <= a, n < nb - 1), scores(kn_ref), NEG_INF)
            pieces += [(s_p, kp_ref), (scores(kc_ref), kc_ref), (s_n, kn_ref)]
        pieces.append((scores(kx_ref), kx_ref))

        def lane_blocks(s):
            return [s[:, c:c + LANES] for c in range(0, s.shape[1], LANES)]

        mx = functools.reduce(jnp.maximum, [blk for s, _ in pieces for blk in lane_blocks(s)])
        m = jnp.maximum(sink, jnp.max(mx, axis=-1, keepdims=True))
        psum = jnp.zeros((rows, LANES), F32)
        o = jnp.zeros((rows, HEAD_DIM), F32)
        for s, ref in pieces:
            p = jnp.exp(s - m)
            psum = functools.reduce(jnp.add, lane_blocks(p), psum)
            o = o + jnp.dot(p.astype(BF16), head_cols(ref, nkv + h), preferred_element_type=F32)
        o = o / (jnp.exp(sink - m) + jnp.sum(psum, axis=-1, keepdims=True))
        for g in range(grp):
            c = h * grp + g
            o_ref[:, c * HEAD_DIM:(c + 1) * HEAD_DIM] = (
                o[g * ATTN_BLOCK:(g + 1) * ATTN_BLOCK].astype(o_ref.dtype))


def attention(qkv, qkv_ctx, sink, *, nq, nkv, windowed):
    B, L, _ = qkv.shape
    C = qkv_ctx.shape[1]
    nb = L // ATTN_BLOCK
    qd, kvd = nq * HEAD_DIM, nkv * HEAD_DIM
    assert qd % (2 * kvd) == 0 and C % LANES == 0
    kv_col = qd // (2 * kvd)
    q_spec = pl.BlockSpec((None, ATTN_BLOCK, qd), lambda b, n, s: (b, n, 0))

    def kv_spec(off):
        return pl.BlockSpec(
            (None, ATTN_BLOCK, 2 * kvd),
            lambda b, n, s: (b, jnp.clip(n + off, 0, nb - 1), kv_col))

    ctx_spec = pl.BlockSpec((None, C, 2 * kvd), lambda b, n, s: (b, 0, kv_col))
    if windowed:
        in_specs = [q_spec, kv_spec(-1), kv_spec(0), kv_spec(1), ctx_spec]
        args = [qkv, qkv, qkv, qkv, qkv_ctx]
    else:
        in_specs, args = [q_spec, ctx_spec], [qkv, qkv_ctx]
    body = functools.partial(_attn_body, nkv=nkv, grp=nq // nkv, nb=nb, windowed=windowed)
    return pl.pallas_call(
        body,
        grid_spec=pltpu.PrefetchScalarGridSpec(
            num_scalar_prefetch=1, grid=(B, nb), in_specs=in_specs,
            out_specs=pl.BlockSpec((None, ATTN_BLOCK, qd), lambda b, n, s: (b, n, 0))),
        out_shape=jax.ShapeDtypeStruct((B, L, qd), BF16),
        compiler_params=_params("parallel", "parallel"),
        name="attn_win" if windowed else "attn_ctx",
    )(sink.astype(F32), *args)


def rope_tables(L):
    rows = L // GRID_W
    row = jnp.repeat(jnp.arange(rows), GRID_W).astype(F32)
    col = jnp.tile(jnp.arange(GRID_W), rows).astype(F32)
    n_freq = HEAD_DIM // 4
    inv = ROPE_BASE ** (-jnp.arange(n_freq, dtype=F32) / n_freq)
    ar, ac = row[:, None] * inv, col[:, None] * inv
    ang = jnp.concatenate([ar, ar, ac, ac], axis=-1)
    cos, sin = jnp.cos(ang), jnp.sin(ang)
    odd = (jnp.arange(HEAD_DIM) // n_freq) % 2 == 1
    return cos, jnp.where(odd, sin, 0.0), jnp.where(odd, 0.0, -sin)


@functools.lru_cache(maxsize=None)
def _dft_matrices(L):
    N = 2 * L
    fb = min(256, L)
    nfb = L // fb
    k = np.arange(L, dtype=np.int64)[:, None]
    t = np.arange(L, dtype=np.int64)[None, :]
    ang = 2.0 * np.pi * ((k * t) % N).astype(np.float64) / N
    c, s = np.cos(ang), np.sin(ang)
    nyq = np.where(np.arange(L) % 2 == 0, 1.0, -1.0)
    f_re, f_im = c.copy(), -s
    f_im[0] = nyq
    i_re, i_im = 2.0 * c / N, -2.0 * s / N
    i_re[0] = 1.0 / N
    i_im[0] = nyq / N
    fwd = np.concatenate([f_re.reshape(nfb, fb, L), f_im.reshape(nfb, fb, L)], axis=1)
    inv = np.concatenate([i_re.reshape(nfb, fb, L), i_im.reshape(nfb, fb, L)], axis=1)
    inv = np.transpose(inv, (0, 2, 1))
    return fwd.astype(np.float32), inv.astype(np.float32), fb


def _filter_body(z_ref, w1_ref, b1_ref, w2_ref, b2_ref, w3_ref, dec_ref, o_ref, *, n_inner):
    hi = lax.Precision.HIGHEST
    h = jnp.sin(jnp.dot(z_ref[...], w1_ref[...], precision=hi, preferred_element_type=F32)
                + b1_ref[...])
    for i in range(n_inner):
        h = jnp.sin(jnp.dot(h, w2_ref[i], precision=hi, preferred_element_type=F32)
                    + b2_ref[i])
    y = jnp.dot(h, w3_ref[...], precision=hi, preferred_element_type=F32)
    L = y.shape[0]
    t01 = lax.broadcasted_iota(jnp.int32, y.shape, 0).astype(F32) / L
    o_ref[...] = y * (jnp.exp(-t01 * jnp.abs(dec_ref[...])) + HY_MOD_SHIFT)


def hyena_filters(L, f_w1, f_b1, f_w2, f_b2, f_w3, decay):
    order, D = decay.shape
    hid = f_w1.shape[1]
    n_inner = f_w2.shape[0]
    n_dir = f_w3.shape[1] // (order * D)
    t = np.arange(L, dtype=np.float32)
    bands = np.arange(1, HY_EMB_BANDS + 1, dtype=np.float32)
    ang = (np.float32(2.0 * math.pi) * t[:, None] * bands[None, :] / np.float32(L)).astype(np.float32)
    z = np.concatenate([(t / np.float32(L))[:, None], np.cos(ang), np.sin(ang)], axis=-1)
    emb = z.shape[1]
    emb_pad = -(-emb // 8) * 8
    z = np.pad(z, ((0, 0), (0, emb_pad - emb))).astype(np.float32)
    w1 = jnp.pad(f_w1, ((0, emb_pad - emb), (0, 0)))
    tn = _tile(D, 1024)
    per_o = n_dir * D // tn
    full = lambda shape: pl.BlockSpec(shape, lambda j: (0,) * len(shape))
    out = pl.pallas_call(
        functools.partial(_filter_body, n_inner=n_inner),
        grid=(order * n_dir * D // tn,),
        in_specs=[full((L, emb_pad)), full((emb_pad, hid)), full((1, hid)),
                  full((n_inner, hid, hid)), full((n_inner, 1, hid)),
                  pl.BlockSpec((hid, tn), lambda j: (0, j)),
                  pl.BlockSpec((None, 1, tn), lambda j: (j // per_o, 0, j % (D // tn)))],
        out_specs=pl.BlockSpec((L, tn), lambda j: (0, j)),
        out_shape=jax.ShapeDtypeStruct((L, order * n_dir * D), F32),
        compiler_params=_params("parallel"),
        name="hyena_filter",
    )(jnp.asarray(z), w1, f_b1.reshape(1, hid), f_w2, f_b2.reshape(n_inner, 1, hid), f_w3,
      decay.reshape(order, 1, D))
    return out.reshape(L, order, n_dir, D)


def hyena_spectra(L, fwd, filt):
    order, D = filt.shape[1], filt.shape[3]
    nfb, fb2, _ = fwd.shape
    f = filt[:, :, 0]
    g = filt[:, :, 1].at[0].set(0.0)
    cols = jnp.stack([f + g, f - g], axis=2).reshape(L, order * 2 * D)
    hi = cols.astype(BF16)
    lo = (cols - hi.astype(F32)).astype(BF16)
    fwd2 = jnp.concatenate([fwd, fwd], axis=-1).reshape(1, nfb * fb2, 2 * L)
    spec = matmul(fwd2, jnp.concatenate([hi, lo], axis=0), tm=1024)
    return spec.reshape(nfb, fb2, order * 2 * D)


def _lconv_body(v_ref, x_ref, fw_ref, iv_ref, ha_ref, hb_ref, nyq_ref, bias_ref, o_ref,
                acc_ref, vb_ref, *, fb):
    f = pl.program_id(2)

    @pl.when(f == 0)
    def _():
        vb_ref[...] = v_ref[...].astype(BF16)
        acc_ref[...] = jnp.zeros_like(acc_ref)

    spec = jnp.dot(fw_ref[...], vb_ref[...], preferred_element_type=F32)
    xr, xi = spec[:fb], spec[fb:]
    a = ha_ref[...]
    dc = jnp.logical_and(lax.broadcasted_iota(jnp.int32, a.shape, 0) == 0, f == 0)
    b = jnp.where(dc, 0.0, hb_ref[...])
    c = jnp.where(dc, nyq_ref[0:1, :], a)
    yr = (xr * a - xi * b).astype(BF16)
    yi = (xr * b + xi * c).astype(BF16)
    acc_ref[...] += (jnp.dot(iv_ref[:, :fb], yr, preferred_element_type=F32)
                     + jnp.dot(iv_ref[:, fb:], yi, preferred_element_type=F32))

    @pl.when(f == pl.num_programs(2) - 1)
    def _():
        o_ref[...] = (x_ref[...] * (acc_ref[...] + v_ref[...] * bias_ref[...])).astype(o_ref.dtype)


def long_conv_gated(src, src_blk, gate, gate_blk, fwd, inv, spec, o, bias, out_dtype):
    B, L, _ = src.shape
    D = bias.shape[0]
    nfb, fb2, _ = fwd.shape
    fb = fb2 // 2
    tc = _tile(D, 512)
    ncb = D // tc
    col = lambda blk: pl.BlockSpec((None, L, tc), lambda c, b, f: (b, 0, blk * ncb + c))

    def hspec(part, grp, fixed_f=None):
        return pl.BlockSpec((None, fb, tc), lambda c, b, f: (f if fixed_f is None else fixed_f,
                                                            part, grp * ncb + c))

    return pl.pallas_call(
        functools.partial(_lconv_body, fb=fb),
        grid=(ncb, B, nfb),
        in_specs=[col(src_blk), col(gate_blk),
                  pl.BlockSpec((None, fb2, L), lambda c, b, f: (f, 0, 0)),
                  pl.BlockSpec((None, L, fb2), lambda c, b, f: (f, 0, 0)),
                  hspec(0, 2 * o), hspec(1, 2 * o + 1), hspec(1, 2 * o, fixed_f=0),
                  pl.BlockSpec((1, tc), lambda c, b, f: (0, c))],
        out_specs=pl.BlockSpec((None, L, tc), lambda c, b, f: (b, 0, c)),
        out_shape=jax.ShapeDtypeStruct((B, L, D), out_dtype),
        scratch_shapes=[pltpu.VMEM((L, tc), F32), pltpu.VMEM((L, tc), BF16)],
        compiler_params=_params("parallel", "parallel", "arbitrary"),
        name="long_conv",
    )(src, gate, fwd, inv, spec, spec, spec, bias.reshape(1, D))


def hyena_mix(a, res, gate, w_in, conv_w, conv_b, w_out, fbias, dft, spec):
    fwd, inv = dft
    u = matmul(a, w_in, mode="conv3", conv_w=conv_w, conv_b=conv_b)
    z = long_conv_gated(u, 0, u, 1, fwd, inv, spec, 0, fbias[0], F32)
    y = long_conv_gated(z, 0, u, 2, fwd, inv, spec, 1, fbias[1], BF16)
    return matmul(y, w_out, mode="res", res=res, gate=gate)


def _topk_rows(s, k, key=None):
    if key is None:
        key = lax.broadcasted_iota(jnp.int32, s.shape, 0)
    big = jnp.iinfo(jnp.int32).max
    vals, keys = [], []
    for _ in range(k):
        m = jnp.max(s, axis=0, keepdims=True)
        i = jnp.min(jnp.where(s == m, key, big), axis=0, keepdims=True)
        vals.append(m)
        keys.append(i)
        s = jnp.where(key == i, -jnp.inf, s)
    return vals, keys


def _route_body(q_ref, k1_ref, k2_ref, idx_ref, g_ref, *, half, n_keys):
    T = q_ref.shape[0]
    K = PEER_TOPK
    assert K == 16
    r8 = lax.broadcasted_iota(jnp.int32, (8, LANES), 0)
    r16 = lax.broadcasted_iota(jnp.int32, (K, LANES), 0)
    for c in range(T // LANES):
        q = q_ref[c * LANES:(c + 1) * LANES, :]
        s1 = lax.dot_general(k1_ref[...], q[:, :half], NT_DIMS, preferred_element_type=F32)
        s2 = lax.dot_general(k2_ref[...], q[:, half:], NT_DIMS, preferred_element_type=F32)
        v1, i1 = _topk_rows(s1, K)
        v2, i2 = _topk_rows(s2, K)
        v1c, i1c = jnp.concatenate(v1, axis=0), jnp.concatenate(i1, axis=0)
        v2c, i2c = jnp.concatenate(v2, axis=0), jnp.concatenate(i2, axis=0)
        cand = [v1[0] + v2c]
        cid = [i1[0] * n_keys + i2c]
        pos = [r16]
        for i in range(1, 8):
            n = K // (i + 1)
            val = v1[i] + v2c[:8]
            cand.append(val if n >= 8 else jnp.where(r8 < n, val, -jnp.inf))
            cid.append(i1[i] * n_keys + i2c[:8])
            pos.append(i * K + r8)
        cand.append(v1c[8:] + v2[0])
        cid.append(i1c[8:] * n_keys + i2[0])
        pos.append((8 + r8) * K)
        cand, cid, pos = (jnp.concatenate(x, axis=0) for x in (cand, cid, pos))
        top, sel = _topk_rows(cand, K, key=pos)
        ids = [jnp.max(jnp.where(pos == p, cid, -1), axis=0, keepdims=True) for p in sel]
        e = [jnp.exp(t - top[0]) for t in top]
        den = e[0]
        for x in e[1:]:
            den = den + x
        idx_ref[:, c * LANES:(c + 1) * LANES] = jnp.concatenate(ids, axis=0)
        g_ref[:, c * LANES:(c + 1) * LANES] = jnp.concatenate(e, axis=0) / den


def peer_route(q, keys1, keys2):
    M = q.shape[0]
    H, n_keys, half = keys1.shape
    T = _tile(M, 512)
    kspec = pl.BlockSpec((None, n_keys, half), lambda i, h: (h, 0, 0))
    ospec = pl.BlockSpec((None, PEER_TOPK, T), lambda i, h: (h, 0, i))
    return pl.pallas_call(
        functools.partial(_route_body, half=half, n_keys=n_keys),
        grid=(M // T, H),
        in_specs=[pl.BlockSpec((T, 2 * half), lambda i, h: (i, h)), kspec, kspec],
        out_specs=[ospec, ospec],
        out_shape=[jax.ShapeDtypeStruct((H, PEER_TOPK, M), jnp.int32),
                   jax.ShapeDtypeStruct((H, PEER_TOPK, M), F32)],
        compiler_params=_params("parallel", "arbitrary"),
        name="peer_route",
    )(q, keys1.astype(BF16), keys2.astype(BF16))


GATHER_GROUP = 4
GATHER_RING = 4
GATHER_SLOTS = GATHER_GROUP * GATHER_RING


def pack_expert_tables(u, v):
    E, D = u.shape
    ub = lax.bitcast_convert_type(u.astype(jnp.bfloat16), jnp.uint16).astype(jnp.uint32)
    vb = lax.bitcast_convert_type(v.astype(jnp.bfloat16), jnp.uint16).astype(jnp.uint32)
    return (ub | (vb << 16)).reshape(E, D // LANES, LANES)


def _gather_body(idx_hbm, wg_ref, x_ref, res_ref, g2_ref, uv_hbm, o_ref, idx_smem, *scratch,
                 n_sel, d):
    slots, (isem, sems) = scratch[:GATHER_SLOTS], scratch[GATHER_SLOTS:]
    sub = d // LANES
    tb = x_ref.shape[0]
    n_idx = tb * n_sel
    step = pl.program_id(0)
    last_step = pl.num_programs(0) - 1
    cur = (step % 2) * n_idx
    nxt = n_idx - cur

    def idx_copy(blk, off):
        return pltpu.make_async_copy(idx_hbm.at[blk], idx_smem.at[pl.ds(off, n_idx)], isem)

    def row_copy(e, j, s):
        return pltpu.make_async_copy(uv_hbm.at[e], slots[s].at[:, j, :], sems.at[s])

    def issue_group(g, ring):
        for k in range(GATHER_GROUP):
            t = g * GATHER_GROUP + k
            beyond = jnp.where(step == last_step, cur + (tb - 1) * n_sel, nxt + (t - tb) * n_sel)
            base = jnp.where(t < tb, cur + t * n_sel, beyond)
            for j in range(n_sel):
                row_copy(idx_smem[base + j], j, ring * GATHER_GROUP + k).start(priority=j % 2)

    def wait_group(ring):
        for k in range(GATHER_GROUP):
            for j in range(n_sel):
                row_copy(0, j, ring * GATHER_GROUP + k).wait()

    ones = jnp.ones((8, LANES), BF16)

    def compute(t, s):
        x = x_ref[pl.ds(t, 1), :]
        p, vs = None, []
        for q in range(sub):
            word = slots[s][q]
            u_q = lax.bitcast_convert_type(word << 16, F32)
            vs.append(lax.bitcast_convert_type(word & jnp.uint32(0xFFFF0000), F32))
            term = u_q * x[:, q * LANES:(q + 1) * LANES]
            p = term if p is None else p + term
        hi = p.astype(BF16)
        lo = (p - hi.astype(F32)).astype(BF16)
        act = (lax.dot_general(ones, hi, NT_DIMS, preferred_element_type=F32)
               + lax.dot_general(ones, lo, NT_DIMS, preferred_element_type=F32))
        gelu = 0.5 * act * (1.0 + lax.erf(act * (2.0 ** -0.5)))
        w = (wg_ref[pl.ds(t, 1), :] * gelu).astype(BF16)
        v = jnp.concatenate(vs, axis=1).astype(BF16)
        out = jnp.dot(w, v, preferred_element_type=F32)
        o_ref[pl.ds(t, 1), :] = res_ref[pl.ds(t, 1), :] + g2_ref[...] * out[0:1]

    @pl.when(step == 0)
    def _():
        first = idx_copy(0, 0)
        first.start()
        first.wait()
        for r in range(GATHER_RING - 1):
            issue_group(r, r)

    @pl.when(step < last_step)
    def _():
        idx_copy(step + 1, nxt).start()

    n_sweeps = tb // GATHER_SLOTS

    def sweep(it, carry):
        @pl.when(jnp.logical_and(it == n_sweeps - 1, step < last_step))
        def _():
            idx_copy(step + 1, nxt).wait()

        for r in range(GATHER_RING):
            g = it * GATHER_RING + r
            wait_group(r)
            issue_group(g + GATHER_RING - 1, (r - 1) % GATHER_RING)
            for k in range(GATHER_GROUP):
                compute(g * GATHER_GROUP + k, r * GATHER_GROUP + k)
        return carry

    lax.fori_loop(0, n_sweeps, sweep, 0)

    @pl.when(step == last_step)
    def _():
        for r in range(GATHER_RING - 1):
            wait_group(r)


def peer_experts(idx, gates, x, res, gate2, uv):
    B, L, D = x.shape
    M = B * L
    n_sel = idx.shape[1]
    tb = _tile(L, 512)
    lb = L // tb
    assert tb % GATHER_SLOTS == 0
    body = functools.partial(_gather_body, n_sel=n_sel, d=D)
    tok = lambda shape_last: pl.BlockSpec((tb, shape_last), lambda i: (i, 0))
    out = pl.pallas_call(
        body,
        grid=(M // tb,),
        in_specs=[pl.BlockSpec(memory_space=pl.ANY), tok(n_sel), tok(D), tok(D),
                  pl.BlockSpec((None, 1, D), lambda i: (i // lb, 0, 0)),
                  pl.BlockSpec(memory_space=pl.ANY)],
        out_specs=tok(D),
        out_shape=jax.ShapeDtypeStruct((M, D), F32),
        scratch_shapes=([pltpu.SMEM((2 * tb * n_sel,), jnp.int32)]
                        + [pltpu.VMEM((D // LANES, n_sel, LANES), jnp.uint32)] * GATHER_SLOTS
                        + [pltpu.SemaphoreType.DMA(()), pltpu.SemaphoreType.DMA((GATHER_SLOTS,))]),
        compiler_params=_params("arbitrary"),
        name="peer_experts",
    )(idx.reshape(M // tb, tb * n_sel), gates, x.reshape(M, D), res.reshape(M, D), gate2, uv)
    return out.reshape(B, L, D)


def peer_mix(h, g, shift, scale, gate2, w_q, keys1, keys2, uv):
    B, L, D = h.shape
    a, a32 = modulate(h, g, shift, scale, out_dtypes=(BF16, F32))
    q = matmul(a, w_q, out_dtype=BF16)
    idx, gates = peer_route(q.reshape(B * L, -1), keys1, keys2)
    n_sel = idx.shape[0] * idx.shape[1]
    idx = idx.reshape(n_sel, B * L).T
    gates = gates.reshape(n_sel, B * L).T
    return peer_experts(idx, gates, a32, h, gate2, uv)


def kernel(x, c, ctx, c_ctx, ada_w, ada_b, norm_mix_g, norm_ffn_g, final_g, attn_w_qkv, attn_w_o, attn_sink, hy_w_in, hy_conv_w, hy_conv_b, hy_f_w1, hy_f_b1, hy_f_w2, hy_f_b2, hy_f_w3, hy_decay, hy_fbias, hy_w_out, peer_w_q, peer_keys1, peer_keys2, peer_u, peer_v):
    B, L, D = x.shape
    C = ctx.shape[1]
    depth = ada_w.shape[0]
    n_mixers = 2
    last_attn = max(i for i in range(depth) if i % n_mixers == 0)
    nq = attn_w_o.shape[1] // HEAD_DIM
    nkv = (attn_w_qkv.shape[2] - nq * HEAD_DIM) // (2 * HEAD_DIM)

    cond = jnp.concatenate([jax.nn.silu(c), jax.nn.silu(c_ctx)[None]], axis=0)
    cond = jnp.pad(cond, ((0, -(B + 1) % 16), (0, 0))).astype(BF16)[None]
    rope = rope_tables(L)
    h_lat, h_ctx = x, ctx

    for i in range(depth):
        is_attn = i % n_mixers == 0
        j = i // n_mixers
        ctx_update = i < last_attn
        mods = matmul(cond, ada_w[i])[0, :B + 1] + ada_b[i]
        lat = [m[:, None, :] for m in jnp.split(mods[:B], N_MODS, axis=-1)]
        cx = [jnp.broadcast_to(m[None], (B, 1, D)) for m in jnp.split(mods[B:], N_MODS, axis=-1)]
        sh1, sc1, g1, sh2, sc2, g2 = lat
        csh1, csc1, cg1, csh2, csc2, cg2 = cx
        a_lat = modulate(h_lat, norm_mix_g[i], sh1, sc1)

        if is_attn:
            w_qkv, w_o = attn_w_qkv[j], attn_w_o[j]
            a_ctx = modulate(h_ctx, norm_mix_g[i], csh1, csc1)
            qkv = matmul(a_lat, w_qkv, mode="rope", out_dtype=BF16, rope=rope,
                         n_rope_cols=(nq + nkv) * HEAD_DIM)
            qkv_ctx = matmul(a_ctx, w_qkv, out_dtype=BF16)
            o_lat = attention(qkv, qkv_ctx, attn_sink[j], nq=nq, nkv=nkv, windowed=True)
            h_lat = matmul(o_lat, w_o, mode="res", res=h_lat, gate=g1)
            if ctx_update:
                o_ctx = attention(qkv_ctx, qkv_ctx, attn_sink[j], nq=nq, nkv=nkv, windowed=False)
                h_ctx = matmul(o_ctx, w_o, mode="res", res=h_ctx, gate=cg1)
        else:
            w_in, w_out = hy_w_in[j], hy_w_out[j]
            f_args = (hy_f_w1[j], hy_f_b1[j], hy_f_w2[j], hy_f_b2[j], hy_f_w3[j], hy_decay[j])
            seqs = [(a_lat, h_lat, g1)]
            if ctx_update:
                seqs.append((modulate(h_ctx, norm_mix_g[i], csh1, csc1), h_ctx, cg1))
            outs = []
            for a_seq, h_seq, gate in seqs:
                Ls = a_seq.shape[1]
                fwd, inv, _ = _dft_matrices(Ls)
                fwd, inv = jnp.asarray(fwd, BF16), jnp.asarray(inv, BF16)
                spec = hyena_spectra(Ls, fwd, hyena_filters(Ls, *f_args))
                outs.append(hyena_mix(a_seq, h_seq, gate, w_in, hy_conv_w[j], hy_conv_b[j],
                                      w_out, hy_fbias[j], (fwd, inv), spec))
            h_lat = outs[0]
            if ctx_update:
                h_ctx = outs[1]

        w_q = peer_w_q[i]
        uv = pack_expert_tables(peer_u[i], peer_v[i])
        h_lat = peer_mix(h_lat, norm_ffn_g[i], sh2, sc2, g2, w_q, peer_keys1[i], peer_keys2[i], uv)
        if ctx_update:
            h_ctx = peer_mix(h_ctx, norm_ffn_g[i], csh2, csc2, cg2, w_q, peer_keys1[i],
                             peer_keys2[i], uv)

    zero = jnp.zeros((B, 1, D), F32)
    return modulate(h_lat, final_g, zero, zero, out_dtypes=(F32,))
```
